```python
import jax, jax.numpy as jnp
from jax import lax
import numpy as np

D_MODEL = 1024
BATCH = 8
SEQ = 2048
DEPTH = 4

N_MIXERS = 3
ROPE_THETA = 500000.0
NORM_EPS = 1e-6
BLOCK = 128
NEG_INF = -1e30
SWA_HEADS = 16
SWA_KV_HEADS = 4
SWA_HEAD_DIM = D_MODEL // SWA_HEADS
SWA_GROUP = SWA_HEADS // SWA_KV_HEADS
SWA_WINDOW = 128
SWA_ROT = SWA_HEAD_DIM // 4
RWKV_HEAD_DIM = 64
RWKV_HEADS = D_MODEL // RWKV_HEAD_DIM
RWKV_DECAY_LORA = 64
RWKV_A_LORA = 64
RWKV_GATE_LORA = 160
RWKV_GN_EPS = 64e-5
MLA_HEADS = 16
MLA_NOPE = 64
MLA_ROPE = 32
MLA_V = 64
MLA_Q_LORA = 384
MLA_KV_LORA = 256
FFN_DIM = 2816
CONV_WIDTH = 3

kernel_name = 'hybrid_swa_rwkv7_mla_convffn'

F32 = jnp.float32


def rms_norm(x, g):
    xf = x.astype(F32)
    y = xf * lax.rsqrt(jnp.mean(xf * xf, axis=-1, keepdims=True) + NORM_EPS)
    return (y * g.astype(F32)).astype(x.dtype)


def rope_tables(seq, rot_dim):
    inv = ROPE_THETA ** (-jnp.arange(0, rot_dim, 2, dtype=F32) / rot_dim)
    ang = jnp.arange(seq, dtype=F32)[:, None] * inv[None, :]
    return jnp.cos(ang), jnp.sin(ang)


def rope_slice(x, cos, sin, start):
    rot = 2 * cos.shape[-1]
    half = rot // 2
    x1 = x[..., start:start + half]
    x2 = x[..., start + half:start + rot]
    c = cos[None, :, None, :].astype(x.dtype)
    s = sin[None, :, None, :].astype(x.dtype)
    return jnp.concatenate([x[..., :start], x1 * c - x2 * s, x2 * c + x1 * s, x[..., start + rot:]], axis=-1)


def swa_mixer(h, w_qkv, q_gain, k_gain, sinks, w_o, cos, sin):
    b, s, _ = h.shape
    nb = s // BLOCK
    qd = SWA_HEADS * SWA_HEAD_DIM
    kd = SWA_KV_HEADS * SWA_HEAD_DIM
    qkv = h @ w_qkv
    q = qkv[..., :qd].reshape(b, s, SWA_HEADS, SWA_HEAD_DIM)
    k = qkv[..., qd:qd + kd].reshape(b, s, SWA_KV_HEADS, SWA_HEAD_DIM)
    v = qkv[..., qd + kd:].reshape(b, s, SWA_KV_HEADS, SWA_HEAD_DIM)
    q = rope_slice(rms_norm(q, q_gain), cos, sin, 0)
    k = rope_slice(rms_norm(k, k_gain), cos, sin, 0)
    q = q.reshape(b, nb, BLOCK, SWA_KV_HEADS, SWA_GROUP, SWA_HEAD_DIM)

    def band(t):
        tp = jnp.pad(t, ((0, 0), (BLOCK, BLOCK), (0, 0), (0, 0)))
        tp = tp.reshape(b, nb + 2, BLOCK, SWA_KV_HEADS, SWA_HEAD_DIM)
        return jnp.concatenate([tp[:, :-2], tp[:, 1:-1], tp[:, 2:]], axis=2)

    kb, vb = band(k), band(v)
    blk = jnp.arange(nb)[:, None]
    qpos = blk * BLOCK + jnp.arange(BLOCK)[None, :]
    kpos = (blk - 1) * BLOCK + jnp.arange(3 * BLOCK)[None, :]
    valid = ((kpos >= 0) & (kpos < s))[:, None, :]
    mask = (jnp.abs(qpos[:, :, None] - kpos[:, None, :]) <= SWA_WINDOW) & valid
    sink = sinks.astype(F32).reshape(SWA_KV_HEADS, SWA_GROUP)[None, :, :, None, None]
    scale = SWA_HEAD_DIM ** -0.5

    def attend(args):
        qb, kbb, vbb, mb = args
        sc = jnp.einsum('bqhgd,bkhd->bhgqk', qb, kbb).astype(F32) * scale
        sc = jnp.where(mb, sc, NEG_INF)
        m = jnp.maximum(jnp.max(sc, axis=-1, keepdims=True), sink)
        p = jnp.exp(sc - m)
        p = p / (jnp.sum(p, axis=-1, keepdims=True) + jnp.exp(sink - m))
        return jnp.einsum('bhgqk,bkhd->bqhgd', p.astype(vbb.dtype), vbb)

    o = lax.map(attend, (jnp.moveaxis(q, 1, 0), jnp.moveaxis(kb, 1, 0), jnp.moveaxis(vb, 1, 0), mask))
    o = jnp.moveaxis(o, 0, 1).reshape(b, s, qd)
    return o @ w_o


def wkv7_scan(r, w, k, v, a, bb, reverse):
    b, s, nh, n = r.shape
    seq = tuple(jnp.moveaxis(t, 1, 0) for t in (r, w, k, v, a, bb))

    def step(state, inp):
        r_t, w_t, k_t, v_t, a_t, b_t = inp
        sa = jnp.einsum('bhvk,bhk->bhv', state, a_t)
        state = state * w_t[:, :, None, :] + sa[..., None] * b_t[:, :, None, :] + v_t[..., None] * k_t[:, :, None, :]
        y = jnp.einsum('bhvk,bhk->bhv', state, r_t)
        return state, y

    s0 = jnp.zeros((b, nh, n, n), F32)
    _, y = lax.scan(step, s0, seq, reverse=reverse)
    return jnp.moveaxis(y, 0, 1)


def rwkv7_mixer(h, mu, w_r, w_k, w_v, w0, w1, w2, a0, a1, a2, g1, g2, k_k, k_a, r_k, lnx_w, lnx_b, w_o):
    b, s, d = h.shape
    hp = jnp.pad(h, ((0, 0), (1, 1), (0, 0)))
    xx = 0.5 * (hp[:, :-2] + hp[:, 2:]) - h
    xr = h + xx * mu[0]
    xw = h + xx * mu[1]
    xk = h + xx * mu[2]
    xv = h + xx * mu[3]
    xa = h + xx * mu[4]
    xg = h + xx * mu[5]
    r = xr @ w_r
    k = xk @ w_k
    v = xv @ w_v
    g = jax.nn.sigmoid(xg @ g1) @ g2

    def heads(t):
        return t.reshape(b, s, RWKV_HEADS, RWKV_HEAD_DIM)

    kk = heads(k * k_k).astype(F32)
    kk = kk / jnp.maximum(jnp.sqrt(jnp.sum(kk * kk, axis=-1, keepdims=True)), 1e-12)
    rf = heads(r).astype(F32)
    vf = heads(v).astype(F32)
    kf = k.astype(F32)
    rkf = r_k.astype(F32)

    def direction(dirn):
        wl = (w0[dirn] + jnp.tanh(xw @ w1[dirn]) @ w2[dirn]).astype(F32)
        decay = jnp.exp(-jnp.exp(-jax.nn.softplus(-wl) - 0.5))
        a = jax.nn.sigmoid((a0[dirn] + (xa @ a1[dirn]) @ a2[dirn]).astype(F32))
        kd = kf * (1.0 + (a - 1.0) * k_a.astype(F32))
        a, kd, decay = heads(a), heads(kd), heads(decay)
        y = wkv7_scan(rf, decay, kd, vf, -kk, kk * a, reverse=(dirn == 1))
        bonus = jnp.sum(rf * kd * rkf, axis=-1, keepdims=True) * vf
        return y, bonus

    y_f, bonus_f = direction(0)
    y_b, bonus_b = direction(1)
    y = y_f + y_b
    mean = jnp.mean(y, axis=-1, keepdims=True)
    var = jnp.mean(jnp.square(y - mean), axis=-1, keepdims=True)
    yn = ((y - mean) * lax.rsqrt(var + RWKV_GN_EPS)).reshape(b, s, d) * lnx_w.astype(F32) + lnx_b.astype(F32)
    out = (yn + (bonus_f + bonus_b).reshape(b, s, d)) * g.astype(F32)
    return out.astype(h.dtype) @ w_o


def mla_mixer(h, w_down, cq_gain, ckv_gain, w_uq, w_ukv, q_gain, k_gain, w_o, cos, sin):
    b, s, _ = h.shape
    nb = s // BLOCK
    down = h @ w_down
    cq = rms_norm(down[..., :MLA_Q_LORA], cq_gain)
    ckv = rms_norm(down[..., MLA_Q_LORA:MLA_Q_LORA + MLA_KV_LORA], ckv_gain)
    k_rope = down[..., MLA_Q_LORA + MLA_KV_LORA:]
    q = (cq @ w_uq).reshape(b, s, MLA_HEADS, MLA_NOPE + MLA_ROPE)
    kv = (ckv @ w_ukv).reshape(b, s, MLA_HEADS, MLA_NOPE + MLA_V)
    k = jnp.concatenate([kv[..., :MLA_NOPE], jnp.broadcast_to(k_rope[:, :, None, :], (b, s, MLA_HEADS, MLA_ROPE))], axis=-1)
    v = kv[..., MLA_NOPE:]
    q = rope_slice(rms_norm(q, q_gain), cos, sin, MLA_NOPE)
    k = rope_slice(rms_norm(k, k_gain), cos, sin, MLA_NOPE)
    scale = (MLA_NOPE + MLA_ROPE) ** -0.5
    qb = jnp.moveaxis(q.reshape(b, nb, BLOCK, MLA_HEADS, MLA_NOPE + MLA_ROPE), 1, 0)

    def attend(qblk):
        sc = jnp.einsum('bqhd,bkhd->bhqk', qblk, k).astype(F32) * scale
        p = jax.nn.softmax(sc, axis=-1)
        return jnp.einsum('bhqk,bkhd->bqhd', p.astype(v.dtype), v)

    o = lax.map(attend, qb)
    o = jnp.moveaxis(o, 0, 1).reshape(b, s, MLA_HEADS * MLA_V)
    return o @ w_o


def conv_ffn(h, w_up, conv_w, conv_b, w_down):
    s = h.shape[1]
    pad = CONV_WIDTH // 2
    u = h @ w_up
    up = jnp.pad(u, ((0, 0), (pad, pad), (0, 0)))
    acc = up[:, :s] * conv_w[0] + conv_b
    for t in range(1, CONV_WIDTH):
        acc = acc + up[:, t:t + s] * conv_w[t]
    gate, val = jnp.split(acc, 2, axis=-1)
    return (jax.nn.silu(gate) * val) @ w_down


def setup_inputs(seed: int = 0) -> dict:
    key = jax.random.key(seed)
    ks = iter(jax.random.split(key, 48))
    n_a = len(range(0, DEPTH, N_MIXERS))
    n_b = len(range(1, DEPTH, N_MIXERS))
    n_c = len(range(2, DEPTH, N_MIXERS))
    d = D_MODEL

    def nrm(shape, scale):
        return scale * jax.random.normal(next(ks), shape, F32)

    def unif(shape, lo, hi):
        return jax.random.uniform(next(ks), shape, F32, lo, hi)

    qkv_w = (SWA_HEADS + 2 * SWA_KV_HEADS) * SWA_HEAD_DIM
    return {
        'x': nrm((BATCH, SEQ, d), 1.0),
        'norm_tok': 1.0 + nrm((DEPTH, d), 0.05),
        'norm_ch': 1.0 + nrm((DEPTH, d), 0.05),
        'ffn_w_up': nrm((DEPTH, d, 2 * FFN_DIM), d ** -0.5),
        'ffn_conv_w': nrm((DEPTH, CONV_WIDTH, 2 * FFN_DIM), CONV_WIDTH ** -0.5),
        'ffn_conv_b': nrm((DEPTH, 2 * FFN_DIM), 0.02),
        'ffn_w_down': nrm((DEPTH, FFN_DIM, d), FFN_DIM ** -0.5),
        'swa_w_qkv': nrm((n_a, d, qkv_w), d ** -0.5),
        'swa_q_gain': 1.0 + nrm((n_a, SWA_HEAD_DIM), 0.05),
        'swa_k_gain': 1.0 + nrm((n_a, SWA_HEAD_DIM), 0.05),
        'swa_sinks': nrm((n_a, SWA_HEADS), 1.0),
        'swa_w_o': nrm((n_a, SWA_HEADS * SWA_HEAD_DIM, d), d ** -0.5),
        'rwkv_mu': unif((n_b, 6, d), 0.0, 1.0),
        'rwkv_w_r': nrm((n_b, d, d), d ** -0.5),
        'rwkv_w_k': nrm((n_b, d, d), d ** -0.5),
        'rwkv_w_v': nrm((n_b, d, d), d ** -0.5),
        'rwkv_w0': unif((n_b, 2, d), -6.0, -1.0),
        'rwkv_w1': nrm((n_b, 2, d, RWKV_DECAY_LORA), d ** -0.5),
        'rwkv_w2': nrm((n_b, 2, RWKV_DECAY_LORA, d), 0.5 * RWKV_DECAY_LORA ** -0.5),
        'rwkv_a0': nrm((n_b, 2, d), 0.5),
        'rwkv_a1': nrm((n_b, 2, d, RWKV_A_LORA), d ** -0.5),
        'rwkv_a2': nrm((n_b, 2, RWKV_A_LORA, d), 0.5 * RWKV_A_LORA ** -0.5),
        'rwkv_g1': nrm((n_b, d, RWKV_GATE_LORA), d ** -0.5),
        'rwkv_g2': nrm((n_b, RWKV_GATE_LORA, d), RWKV_GATE_LORA ** -0.5),
        'rwkv_k_k': 0.85 + nrm((n_b, d), 0.1),
        'rwkv_k_a': 1.0 + nrm((n_b, d), 0.1),
        'rwkv_r_k': nrm((n_b, RWKV_HEADS, RWKV_HEAD_DIM), 0.1),
        'rwkv_lnx_w': 1.0 + nrm((n_b, d), 0.05),
        'rwkv_lnx_b': nrm((n_b, d), 0.01),
        'rwkv_w_o': nrm((n_b, d, d), d ** -0.5),
        'mla_w_down': nrm((n_c, d, MLA_Q_LORA + MLA_KV_LORA + MLA_ROPE), d ** -0.5),
        'mla_cq_gain': 1.0 + nrm((n_c, MLA_Q_LORA), 0.05),
        'mla_ckv_gain': 1.0 + nrm((n_c, MLA_KV_LORA), 0.05),
        'mla_w_uq': nrm((n_c, MLA_Q_LORA, MLA_HEADS * (MLA_NOPE + MLA_ROPE)), MLA_Q_LORA ** -0.5),
        'mla_w_ukv': nrm((n_c, MLA_KV_LORA, MLA_HEADS * (MLA_NOPE + MLA_V)), MLA_KV_LORA ** -0.5),
        'mla_q_gain': 1.0 + nrm((n_c, MLA_NOPE + MLA_ROPE), 0.05),
        'mla_k_gain': 1.0 + nrm((n_c, MLA_NOPE + MLA_ROPE), 0.05),
        'mla_w_o': nrm((n_c, MLA_HEADS * MLA_V, d), d ** -0.5),
    }


def reference(x, norm_tok, norm_ch, ffn_w_up, ffn_conv_w, ffn_conv_b, ffn_w_down,
              swa_w_qkv, swa_q_gain, swa_k_gain, swa_sinks, swa_w_o,
              rwkv_mu, rwkv_w_r, rwkv_w_k, rwkv_w_v, rwkv_w0, rwkv_w1, rwkv_w2,
              rwkv_a0, rwkv_a1, rwkv_a2, rwkv_g1, rwkv_g2, rwkv_k_k, rwkv_k_a, rwkv_r_k,
              rwkv_lnx_w, rwkv_lnx_b, rwkv_w_o,
              mla_w_down, mla_cq_gain, mla_ckv_gain, mla_w_uq, mla_w_ukv,
              mla_q_gain, mla_k_gain, mla_w_o):
    s = x.shape[1]
    cos_a, sin_a = rope_tables(s, SWA_ROT)
    cos_c, sin_c = rope_tables(s, MLA_ROPE)
    for i in range(DEPTH):
        kind = i % N_MIXERS
        j = i // N_MIXERS
        h = rms_norm(x, norm_tok[i])
        if kind == 0:
            y = swa_mixer(h, swa_w_qkv[j], swa_q_gain[j], swa_k_gain[j], swa_sinks[j], swa_w_o[j], cos_a, sin_a)
        elif kind == 1:
            y = rwkv7_mixer(h, rwkv_mu[j], rwkv_w_r[j], rwkv_w_k[j], rwkv_w_v[j], rwkv_w0[j], rwkv_w1[j], rwkv_w2[j],
                            rwkv_a0[j], rwkv_a1[j], rwkv_a2[j], rwkv_g1[j], rwkv_g2[j], rwkv_k_k[j], rwkv_k_a[j],
                            rwkv_r_k[j], rwkv_lnx_w[j], rwkv_lnx_b[j], rwkv_w_o[j])
        else:
            y = mla_mixer(h, mla_w_down[j], mla_cq_gain[j], mla_ckv_gain[j], mla_w_uq[j], mla_w_ukv[j],
                          mla_q_gain[j], mla_k_gain[j], mla_w_o[j], cos_c, sin_c)
        x = x + y
        x = x + conv_ffn(rms_norm(x, norm_ch[i]), ffn_w_up[i], ffn_conv_w[i], ffn_conv_b[i], ffn_w_down[i])
    return x
```

```python
import functools

import jax
import jax.numpy as jnp
from jax import lax
from jax.experimental import pallas as pl
from jax.experimental.pallas import tpu as pltpu

F32 = jnp.float32
BF16 = jnp.bfloat16

N_MIXERS = 3
ROPE_THETA = 500000.0
NORM_EPS = 1e-6
NEG_INF = -1e30
LANES = 128
HEAD_DIM = 64
SWA_HEADS = 16
SWA_KV_HEADS = 4
SWA_WINDOW = 128
SWA_BLOCK = 128
SWA_ROT = HEAD_DIM // 4
RWKV_GN_EPS = 64e-5
RWKV_CHUNK = 64
MLA_HEADS = 16
MLA_NOPE = 64
MLA_ROPE = 32
MLA_V = 64
MLA_Q_LORA = 384
MLA_KV_LORA = 256
MLA_DOWN_PAD = 768
VMEM_LIMIT_BYTES = 56 * 1024 * 1024

NN = ((1,), (0,))
NT = ((1,), (1,))
TN = ((0,), (0,))


def _dot(a, b, dims=NN):
    return lax.dot_general(a, b, (dims, ((), ())), preferred_element_type=F32)


def _split3(x):
    hi = x.astype(BF16)
    r1 = x - hi.astype(F32)
    mid = r1.astype(BF16)
    lo = (r1 - mid.astype(F32)).astype(BF16)
    return hi, mid, lo


def _mm(a, b, dims=NN, passes=1):
    if passes == 1:
        return _dot(a.astype(BF16), b.astype(BF16), dims)
    ah, am, _ = _split3(a)
    bh, bm, _ = _split3(b)
    return _dot(ah, bh, dims) + (_dot(ah, bm, dims) + _dot(am, bh, dims))


def _mm_exact_lhs(a01, b):
    bh, bm, bl = _split3(b)
    a = a01.astype(BF16)
    return _dot(a, bh) + (_dot(a, bm) + _dot(a, bl))


def _rms(x, eps=NORM_EPS):
    return x * lax.rsqrt(jnp.mean(x * x, axis=-1, keepdims=True) + eps)


def _low_half(shape):
    return lax.broadcasted_iota(jnp.int32, shape, 1) < HEAD_DIM


def _pair_sum(x, m0):
    s0 = jnp.sum(jnp.where(m0, x, 0.0), axis=-1, keepdims=True)
    s1 = jnp.sum(jnp.where(m0, 0.0, x), axis=-1, keepdims=True)
    return jnp.where(m0, s0, s1)


def _params(*sem):
    return pltpu.CompilerParams(dimension_semantics=sem, vmem_limit_bytes=VMEM_LIMIT_BYTES)


def _tiles(rows):
    def pick(pref):
        t = pref
        while rows % t:
            t //= 2
        return t
    return {"proj": pick(512), "rwkv_proj": pick(256), "mla_proj": pick(256), "mla_q": pick(256)}


def _ffn_kernel(x_ref, g_ref, wg_ref, wv_ref, cwg_ref, cwv_ref, cbg_ref, cbv_ref, wd_ref, o_ref, hn_ref):
    j = pl.program_id(1)

    @pl.when(j == 0)
    def _():
        x = x_ref[0]
        hn_ref[...] = (_rms(x) * g_ref[...]).astype(BF16)
        o_ref[0] = x

    hn = hn_ref[...]
    s = hn.shape[0]
    row = lax.broadcasted_iota(jnp.int32, (s, wg_ref.shape[1]), 0)

    def conv(u, cw_ref, cb_ref):
        prev = jnp.where(row == 0, 0.0, pltpu.roll(u, 1, 0))
        nxt = jnp.where(row == s - 1, 0.0, pltpu.roll(u, s - 1, 0))
        return prev * cw_ref[0:1, :] + cb_ref[...] + u * cw_ref[1:2, :] + nxt * cw_ref[2:3, :]

    gate = conv(_dot(hn, wg_ref[...]), cwg_ref, cbg_ref)
    val = conv(_dot(hn, wv_ref[...]), cwv_ref, cbv_ref)
    act = (gate * jax.nn.sigmoid(gate) * val).astype(BF16)
    o_ref[0] += _dot(act, wd_ref[...])


def _ffn(x, g, w_up, conv_w, conv_b, w_down, tf=256):
    b, s, d = x.shape
    f = w_down.shape[0]
    nf = f // tf
    wu = w_up.astype(BF16)
    wd = w_down.astype(BF16)
    cb = conv_b.reshape(1, 2 * f)
    return pl.pallas_call(
        _ffn_kernel,
        grid=(b, nf),
        in_specs=[
            pl.BlockSpec((1, s, d), lambda i, j: (i, 0, 0)),
            pl.BlockSpec((1, d), lambda i, j: (0, 0)),
            pl.BlockSpec((d, tf), lambda i, j: (0, j)),
            pl.BlockSpec((d, tf), lambda i, j: (0, nf + j)),
            pl.BlockSpec((3, tf), lambda i, j: (0, j)),
            pl.BlockSpec((3, tf), lambda i, j: (0, nf + j)),
            pl.BlockSpec((1, tf), lambda i, j: (0, j)),
            pl.BlockSpec((1, tf), lambda i, j: (0, nf + j)),
            pl.BlockSpec((tf, d), lambda i, j: (j, 0)),
        ],
        out_specs=pl.BlockSpec((1, s, d), lambda i, j: (i, 0, 0)),
        out_shape=jax.ShapeDtypeStruct((b, s, d), F32),
        scratch_shapes=[pltpu.VMEM((s, d), BF16)],
        compiler_params=_params("parallel", "arbitrary"),
        name="conv_ffn",
    )(x, g.reshape(1, d), wu, wu, conv_w, conv_w, cb, cb, wd)


def _proj_res_kernel(a_ref, w_ref, x_ref, o_ref):
    o_ref[...] = x_ref[...] + _dot(a_ref[...], w_ref[...])


def _proj_res(a, w, x):
    t, k = a.shape
    d = w.shape[1]
    tm = _tiles(t)["proj"]
    return pl.pallas_call(
        _proj_res_kernel,
        grid=(t // tm,),
        in_specs=[
            pl.BlockSpec((tm, k), lambda i: (i, 0)),
            pl.BlockSpec((k, d), lambda i: (0, 0)),
            pl.BlockSpec((tm, d), lambda i: (i, 0)),
        ],
        out_specs=pl.BlockSpec((tm, d), lambda i: (i, 0)),
        out_shape=jax.ShapeDtypeStruct((t, d), F32),
        compiler_params=_params("parallel"),
        name="proj_residual",
    )(a, w.astype(BF16), x)


def _rope_apply(xn, cos_t, sin_up, sin_dn, half):
    return xn * cos_t + pltpu.roll(xn, LANES - half, 1) * sin_up + pltpu.roll(xn, half, 1) * sin_dn


def _rope_tables(s, rot, start, group):
    half = rot // 2
    inv = ROPE_THETA ** (-jnp.arange(0, rot, 2, dtype=F32) / rot)
    ang = jnp.arange(s, dtype=F32)[:, None] * inv[None, :]
    cos, sin = jnp.cos(ang), jnp.sin(ang)
    lane = jnp.arange(LANES) % group
    first = (lane >= start) & (lane < start + half)
    second = (lane >= start + half) & (lane < start + rot)
    idx = jnp.clip(jnp.where(second, lane - start - half, lane - start), 0, half - 1)
    cos_t = jnp.where(first | second, cos[:, idx], 1.0)
    sin_up = jnp.where(first, -sin[:, idx], 0.0)
    sin_dn = jnp.where(second, sin[:, idx], 0.0)
    return cos_t, sin_up, sin_dn


def _swa_qkv_kernel(x_ref, g_ref, w_ref, qg_ref, kg_ref, c_ref, su_ref, sd_ref, q_ref, k_ref, v_ref):
    hn = (_rms(x_ref[...]) * g_ref[...]).astype(BF16)
    y = _dot(hn, w_ref[...])
    tm = y.shape[0]
    nq = q_ref.shape[1]
    nk = k_ref.shape[1]
    m0 = _low_half((tm, LANES))
    cos_t, sin_up, sin_dn = c_ref[...], su_ref[...], sd_ref[...]

    def head_norm_rope(xc, gain):
        ss = _pair_sum(xc * xc, m0)
        xn = xc * lax.rsqrt(ss * (1.0 / HEAD_DIM) + NORM_EPS) * gain
        return _rope_apply(xn, cos_t, sin_up, sin_dn, SWA_ROT // 2)

    for c in range(nq // LANES):
        sl = slice(c * LANES, (c + 1) * LANES)
        q_ref[:, sl] = head_norm_rope(y[:, sl], qg_ref[...]).astype(BF16)
    for c in range(nk // LANES):
        sl = slice(c * LANES, (c + 1) * LANES)
        k_ref[:, sl] = head_norm_rope(y[:, nq + c * LANES:nq + (c + 1) * LANES], kg_ref[...]).astype(BF16)
    v_ref[...] = y[:, nq + nk:].astype(BF16)


def _swa_attn_kernel(sink_ref, q_ref, k_ref, v_ref, o_ref):
    i = pl.program_id(1)
    s = k_ref.shape[1]
    blk = q_ref.shape[1]
    band = 3 * blk
    start = pl.multiple_of(jnp.clip((i - 1) * blk, 0, s - band), blk)
    group = SWA_HEADS // SWA_KV_HEADS
    rows = group * blk
    qpos = i * blk + lax.broadcasted_iota(jnp.int32, (rows, band), 0) % blk
    kpos = start + lax.broadcasted_iota(jnp.int32, (rows, band), 1)
    mask = jnp.abs(qpos - kpos) <= SWA_WINDOW
    m0q = _low_half((blk, LANES))
    rowblk = lax.broadcasted_iota(jnp.int32, (rows, 1), 0) // blk
    zq = jnp.zeros((blk, LANES), BF16)
    for g in range(SWA_KV_HEADS):
        parts = []
        for p in range(group // 2):
            qp = q_ref[0, :, (g * (group // 2) + p) * LANES:(g * (group // 2) + p + 1) * LANES]
            parts += [jnp.where(m0q, qp, zq), jnp.where(m0q, zq, qp)]
        qs = jnp.concatenate(parts, axis=0)
        kb = k_ref[0, pl.ds(start, band), g * LANES:(g + 1) * LANES]
        vb = v_ref[0, pl.ds(start, band), g * LANES:(g + 1) * LANES]
        sc = jnp.where(mask, _dot(qs, kb, NT), NEG_INF)
        sink = jnp.zeros((rows, 1), F32)
        for jh in range(group):
            sink = jnp.where(rowblk == jh, sink_ref[g * group + jh], sink)
        mx = jnp.maximum(jnp.max(sc, axis=-1, keepdims=True), sink)
        pe = jnp.exp(sc - mx)
        den = jnp.sum(pe, axis=-1, keepdims=True) + jnp.exp(sink - mx)
        pn = (pe / den).astype(BF16)
        ob = _dot(pn, vb)
        for p in range(group // 2):
            o0 = ob[(2 * p) * blk:(2 * p + 1) * blk]
            o1 = ob[(2 * p + 1) * blk:(2 * p + 2) * blk]
            col = (g * (group // 2) + p) * LANES
            o_ref[0, :, col:col + LANES] = jnp.where(m0q, o0, o1).astype(BF16)


def _swa_layer(x, g, w_qkv, q_gain, k_gain, sinks, w_o):
    b, s, d = x.shape
    t = b * s
    qd = SWA_HEADS * HEAD_DIM
    kd = SWA_KV_HEADS * HEAD_DIM
    wq = w_qkv[:, :qd]
    wk = w_qkv[:, qd:qd + kd].reshape(d, SWA_KV_HEADS, 1, HEAD_DIM)
    wv = w_qkv[:, qd + kd:].reshape(d, SWA_KV_HEADS, 1, HEAD_DIM)
    dup = lambda w: jnp.broadcast_to(w, (d, SWA_KV_HEADS, 2, HEAD_DIM)).reshape(d, 2 * kd)
    w_all = jnp.concatenate([wq, dup(wk), dup(wv)], axis=1).astype(BF16)
    nk = 2 * kd
    qg = (jnp.tile(q_gain, 2) * HEAD_DIM ** -0.5).reshape(1, LANES)
    kg = jnp.tile(k_gain, 2).reshape(1, LANES)
    cos_t, sin_up, sin_dn = _rope_tables(s, SWA_ROT, 0, HEAD_DIM)
    tm = _tiles(t)["proj"]
    tm = min(tm, s)
    nt = s // tm
    tab = pl.BlockSpec((tm, LANES), lambda i: (i % nt, 0))
    q, k, v = pl.pallas_call(
        _swa_qkv_kernel,
        grid=(t // tm,),
        in_specs=[
            pl.BlockSpec((tm, d), lambda i: (i, 0)),
            pl.BlockSpec((1, d), lambda i: (0, 0)),
            pl.BlockSpec((d, qd + 2 * nk), lambda i: (0, 0)),
            pl.BlockSpec((1, LANES), lambda i: (0, 0)),
            pl.BlockSpec((1, LANES), lambda i: (0, 0)),
            tab, tab, tab,
        ],
        out_specs=[
            pl.BlockSpec((tm, qd), lambda i: (i, 0)),
            pl.BlockSpec((tm, nk), lambda i: (i, 0)),
            pl.BlockSpec((tm, nk), lambda i: (i, 0)),
        ],
        out_shape=[jax.ShapeDtypeStruct((t, qd), BF16), jax.ShapeDtypeStruct((t, nk), BF16),
                   jax.ShapeDtypeStruct((t, nk), BF16)],
        compiler_params=_params("parallel"),
        name="swa_qkv",
    )(x.reshape(t, d), g.reshape(1, d), w_all, qg, kg, cos_t, sin_up, sin_dn)

    nb = s // SWA_BLOCK
    o = pl.pallas_call(
        _swa_attn_kernel,
        grid_spec=pltpu.PrefetchScalarGridSpec(
            num_scalar_prefetch=1,
            grid=(b, nb),
            in_specs=[
                pl.BlockSpec((1, SWA_BLOCK, qd), lambda bi, i, sk: (bi, i, 0)),
                pl.BlockSpec((1, s, nk), lambda bi, i, sk: (bi, 0, 0)),
                pl.BlockSpec((1, s, nk), lambda bi, i, sk: (bi, 0, 0)),
            ],
            out_specs=pl.BlockSpec((1, SWA_BLOCK, qd), lambda bi, i, sk: (bi, i, 0)),
        ),
        out_shape=jax.ShapeDtypeStruct((b, s, qd), BF16),
        compiler_params=_params("parallel", "arbitrary"),
        name="swa_attention",
    )(sinks.astype(F32), q.reshape(b, s, qd), k.reshape(b, s, nk), v.reshape(b, s, nk))
    return _proj_res(o.reshape(t, qd), w_o, x.reshape(t, d)).reshape(b, s, d)


def _mla_proj_kernel(x_ref, g_ref, wd_ref, cqg_ref, ckvg_ref, wuq_ref, wuk_ref, wuv_ref, qg_ref, kg_ref,
                     c_ref, su_ref, sd_ref, q_ref, k_ref, v_ref):
    hn = (_rms(x_ref[...]) * g_ref[...]).astype(BF16)
    down = _dot(hn, wd_ref[...])
    cq = (_rms(down[:, :MLA_Q_LORA]) * cqg_ref[...]).astype(BF16)
    ckv = (_rms(down[:, MLA_Q_LORA:MLA_Q_LORA + MLA_KV_LORA]) * ckvg_ref[...]).astype(BF16)
    tail = down[:, MLA_Q_LORA + MLA_KV_LORA:]
    k_rope = pltpu.roll(tail, MLA_NOPE, 1)
    qp = _dot(cq, wuq_ref[...])
    kp = _dot(ckv, wuk_ref[...])
    v_ref[...] = _dot(ckv, wuv_ref[...]).astype(BF16)
    cos_t, sin_up, sin_dn = c_ref[...], su_ref[...], sd_ref[...]
    inv_dim = 1.0 / (MLA_NOPE + MLA_ROPE)

    def head_norm_rope(xh, gain):
        ss = jnp.sum(xh * xh, axis=-1, keepdims=True)
        xn = xh * lax.rsqrt(ss * inv_dim + NORM_EPS) * gain
        return _rope_apply(xn, cos_t, sin_up, sin_dn, MLA_ROPE // 2)

    for h in range(MLA_HEADS):
        sl = slice(h * LANES, (h + 1) * LANES)
        q_ref[:, sl] = head_norm_rope(qp[:, sl], qg_ref[...]).astype(BF16)
        k_ref[:, sl] = head_norm_rope(kp[:, sl] + k_rope, kg_ref[...]).astype(BF16)


def _mla_attn_kernel(q_ref, k_ref, v_ref, o_ref):
    vv = v_ref[0]
    outs = []
    for m in range(2):
        sl = slice(m * LANES, (m + 1) * LANES)
        sc = _dot(q_ref[0, :, sl], k_ref[0, :, sl], NT)
        pe = jnp.exp(sc - jnp.max(sc, axis=-1, keepdims=True))
        den = jnp.sum(pe, axis=-1, keepdims=True)
        outs.append(_dot(pe.astype(BF16), vv) / den)
    o_ref[0] = jnp.where(_low_half(outs[0].shape), outs[0], outs[1]).astype(BF16)


def _mla_layer(x, g, w_down, cq_gain, ckv_gain, w_uq, w_ukv, q_gain, k_gain, w_o):
    b, s, d = x.shape
    t = b * s
    hd = MLA_NOPE + MLA_ROPE
    n_down = w_down.shape[1]
    wd = jnp.pad(w_down, ((0, 0), (0, MLA_DOWN_PAD - n_down))).astype(BF16)
    pad_heads = lambda w, width: jnp.pad(w, ((0, 0), (0, 0), (0, LANES - width))).reshape(w.shape[0], MLA_HEADS * LANES)
    wuq = pad_heads(w_uq.reshape(MLA_Q_LORA, MLA_HEADS, hd), hd).astype(BF16)
    wkv = w_ukv.reshape(MLA_KV_LORA, MLA_HEADS, MLA_NOPE + MLA_V)
    wuk = pad_heads(wkv[:, :, :MLA_NOPE], MLA_NOPE).astype(BF16)
    wuv = wkv[:, :, MLA_NOPE:].reshape(MLA_KV_LORA, MLA_HEADS * MLA_V).astype(BF16)
    qg = jnp.pad(q_gain * hd ** -0.5, (0, LANES - hd)).reshape(1, LANES)
    kg = jnp.pad(k_gain, (0, LANES - hd)).reshape(1, LANES)
    cos_t, sin_up, sin_dn = _rope_tables(s, MLA_ROPE, MLA_NOPE, LANES)
    tiles = _tiles(t)
    tm = min(tiles["mla_proj"], s)
    nt = s // tm
    nq = MLA_HEADS * LANES
    nv = MLA_HEADS * MLA_V
    full = lambda shape: pl.BlockSpec(shape, lambda i: (0, 0))
    tab = pl.BlockSpec((tm, LANES), lambda i: (i % nt, 0))
    q, k, v = pl.pallas_call(
        _mla_proj_kernel,
        grid=(t // tm,),
        in_specs=[
            pl.BlockSpec((tm, d), lambda i: (i, 0)),
            full((1, d)), full((d, MLA_DOWN_PAD)), full((1, MLA_Q_LORA)), full((1, MLA_KV_LORA)),
            full((MLA_Q_LORA, nq)), full((MLA_KV_LORA, nq)), full((MLA_KV_LORA, nv)),
            full((1, LANES)), full((1, LANES)), tab, tab, tab,
        ],
        out_specs=[
            pl.BlockSpec((tm, nq), lambda i: (i, 0)),
            pl.BlockSpec((tm, nq), lambda i: (i, 0)),
            pl.BlockSpec((tm, nv), lambda i: (i, 0)),
        ],
        out_shape=[jax.ShapeDtypeStruct((t, nq), BF16), jax.ShapeDtypeStruct((t, nq), BF16),
                   jax.ShapeDtypeStruct((t, nv), BF16)],
        compiler_params=_params("parallel"),
        name="mla_proj",
    )(x.reshape(t, d), g.reshape(1, d), wd, cq_gain.reshape(1, -1), ckv_gain.reshape(1, -1), wuq, wuk, wuv,
      qg, kg, cos_t, sin_up, sin_dn)

    tq = min(tiles["mla_q"], s)
    o = pl.pallas_call(
        _mla_attn_kernel,
        grid=(b, MLA_HEADS // 2, s // tq),
        in_specs=[
            pl.BlockSpec((1, tq, 2 * LANES), lambda bi, p, i: (bi, i, p)),
            pl.BlockSpec((1, s, 2 * LANES), lambda bi, p, i: (bi, 0, p)),
            pl.BlockSpec((1, s, LANES), lambda bi, p, i: (bi, 0, p)),
        ],
        out_specs=pl.BlockSpec((1, tq, LANES), lambda bi, p, i: (bi, i, p)),
        out_shape=jax.ShapeDtypeStruct((b, s, nv), BF16),
        compiler_params=_params("parallel", "parallel", "arbitrary"),
        name="mla_attention",
    )(q.reshape(b, s, nq), k.reshape(b, s, nq), v.reshape(b, s, nv))
    return _proj_res(o.reshape(t, nv), w_o, x.reshape(t, d)).reshape(b, s, d)


def _rwkv_proj_kernel(x_ref, xp_ref, xn_ref, g_ref, mu_ref, wr_ref, wk_ref, wv_ref, g1_ref, g2_ref,
                      w1_ref, w2_ref, w0_ref, a1_ref, a2_ref, a0_ref,
                      r_ref, k_ref, v_ref, gt_ref, lwf_ref, lwb_ref, af_ref, ab_ref, *, tiles_per_seq):
    i = pl.program_id(0)
    gain = g_ref[...]
    h = _rms(x_ref[...]) * gain
    tm = h.shape[0]
    first = (i % tiles_per_seq) == 0
    last = (i % tiles_per_seq) == tiles_per_seq - 1
    h_before = jnp.where(first, 0.0, _rms(xp_ref[7:8, :]) * gain)
    h_after = jnp.where(last, 0.0, _rms(xn_ref[0:1, :]) * gain)
    row = lax.broadcasted_iota(jnp.int32, h.shape, 0)
    h_prev = jnp.where(row == 0, h_before, pltpu.roll(h, 1, 0))
    h_next = jnp.where(row == tm - 1, h_after, pltpu.roll(h, tm - 1, 0))
    xx = 0.5 * (h_prev + h_next) - h
    mix = lambda n: (h + xx * mu_ref[n:n + 1, :]).astype(BF16)
    r_ref[...] = _dot(mix(0), wr_ref[...])
    k_ref[...] = _dot(mix(2), wk_ref[...])
    v_ref[...] = _dot(mix(3), wv_ref[...])
    gt_ref[...] = _dot(jax.nn.sigmoid(_dot(mix(5), g1_ref[...])).astype(BF16), g2_ref[...])
    d = h.shape[1]
    wl = w0_ref[...] + _dot(jnp.tanh(_dot(mix(1), w1_ref[...])).astype(BF16), w2_ref[...])
    lw = -jnp.exp(-0.5) * jax.nn.sigmoid(wl)
    lwf_ref[...] = lw[:, :d]
    lwb_ref[...] = lw[:, d:]
    al = jax.nn.sigmoid(a0_ref[...] + _dot(_dot(mix(4), a1_ref[...]).astype(BF16), a2_ref[...]))
    af_ref[...] = al[:, :d]
    ab_ref[...] = al[:, d:]


def _wkv_chunk(r, lw, k, v, asig, k_k, k_a, state, rev):
    c = r.shape[0]
    m0 = _low_half((c, LANES))
    kk = k * k_k
    kk = kk / jnp.maximum(jnp.sqrt(_pair_sum(kk * kk, m0)), 1e-12)
    bvec = kk * asig
    kd = k * (1.0 + (asig - 1.0) * k_a)

    ri = lax.broadcasted_iota(jnp.int32, (c, c), 0)
    ci = lax.broadcasted_iota(jnp.int32, (c, c), 1)
    upto = (ci >= ri) if rev else (ci <= ri)
    before = (ci > ri) if rev else (ci < ri)
    cl = _mm_exact_lhs(jnp.where(upto, 1.0, 0.0), lw)
    tot = cl[0:1, :] if rev else cl[c - 1:c, :]
    g_inv = jnp.exp(-cl)
    g_end = jnp.exp(tot - cl)
    at = -kk * jnp.exp(cl - lw)
    rt = r * jnp.exp(cl)
    bt = bvec * g_inv
    kt = kd * g_inv
    b_end = bvec * g_end
    k_end = kd * g_end

    z = jnp.zeros_like(at)
    lhs = jnp.concatenate([jnp.where(m0, at, z), jnp.where(m0, rt, z),
                           jnp.where(m0, z, at), jnp.where(m0, z, rt)], axis=0)
    sb = _mm(lhs, bt, NT, 3)
    sk = _mm(lhs, kt, NT, 1)
    eye = jnp.where(ri == ci, 1.0, 0.0)
    wa, u0, rp, yp = [], [], [], []
    for hh in range(2):
        a_ab = jnp.where(before, sb[(2 * hh) * c:(2 * hh + 1) * c], 0.0)
        a_rb = jnp.where(upto, sb[(2 * hh + 1) * c:(2 * hh + 2) * c], 0.0)
        a_ak = jnp.where(before, sk[(2 * hh) * c:(2 * hh + 1) * c], 0.0)
        a_rk = jnp.where(upto, sk[(2 * hh + 1) * c:(2 * hh + 2) * c], 0.0)
        inv = eye + a_ab
        pw = a_ab
        n = 2
        while n < c:
            pw = _mm(pw, pw, NN, 3)
            inv = inv + _mm(inv, pw, NN, 3)
            n *= 2
        zz = _mm(inv, jnp.concatenate([at, _mm(a_ak, v)], axis=1), NN, 3)
        gg = _mm(a_rb, zz)
        wa.append(zz[:, :LANES])
        u0.append(zz[:, LANES:])
        rp.append(gg[:, :LANES])
        yp.append(gg[:, LANES:] + _mm(a_rk, v))
    w_a = jnp.where(m0, wa[0], wa[1])
    u_0 = jnp.where(m0, u0[0], u0[1])
    r_eff = rt + jnp.where(m0, rp[0], rp[1])
    y_loc = jnp.where(m0, yp[0], yp[1])

    y = _mm(r_eff, state, NN, 3) + y_loc
    r2 = lax.broadcasted_iota(jnp.int32, (LANES, LANES), 0)
    c2 = lax.broadcasted_iota(jnp.int32, (LANES, LANES), 1)
    same_head = (r2 < HEAD_DIM) == (c2 < HEAD_DIM)
    p_mat = jnp.where(same_head, _mm(b_end, w_a, TN, 3), 0.0) + jnp.where(r2 == c2, jnp.exp(tot), 0.0)
    q_mat = jnp.where(same_head, _mm(jnp.concatenate([b_end, k_end], axis=0),
                                     jnp.concatenate([u_0, v], axis=0), TN), 0.0)
    return y, _mm(p_mat, state, NN, 3) + q_mat


def _rwkv_scan_kernel(rf_ref, kf_ref, vf_ref, rb_ref, kb_ref, vb_ref, lwf_ref, af_ref, lwb_ref, ab_ref,
                      kk_ref, ka_ref, yf_ref, yb_ref, st_ref, *, pairs):
    @pl.when(pl.program_id(2) == 0)
    def _():
        st_ref[...] = jnp.zeros_like(st_ref)

    dirs = ((rf_ref, kf_ref, vf_ref, lwf_ref, af_ref, yf_ref, False),
            (rb_ref, kb_ref, vb_ref, lwb_ref, ab_ref, yb_ref, True))
    for di, (r_ref, k_ref, v_ref, lw_ref, a_ref, y_ref, rev) in enumerate(dirs):
        for p in range(pairs):
            sl = slice(p * LANES, (p + 1) * LANES)
            y, st = _wkv_chunk(r_ref[0, :, sl], lw_ref[0, :, sl], k_ref[0, :, sl], v_ref[0, :, sl],
                               a_ref[0, :, sl], kk_ref[:, sl], ka_ref[:, sl], st_ref[di, p], rev)
            y_ref[0, :, sl] = y
            st_ref[di, p] = st


def _rwkv_out_kernel(x_ref, r_ref, k_ref, v_ref, af_ref, ab_ref, gt_ref, yf_ref, yb_ref,
                     ka_ref, rk_ref, lw_ref, lb_ref, wo_ref, o_ref, act_ref):
    tm = x_ref.shape[0]
    m0 = _low_half((tm, LANES))
    inv_n = 1.0 / HEAD_DIM
    for c in range(x_ref.shape[1] // LANES):
        sl = slice(c * LANES, (c + 1) * LANES)
        r, k, v = r_ref[:, sl], k_ref[:, sl], v_ref[:, sl]
        k_a = ka_ref[:, sl]
        kd_f = k * (1.0 + (af_ref[:, sl] - 1.0) * k_a)
        kd_b = k * (1.0 + (ab_ref[:, sl] - 1.0) * k_a)
        rr = r * rk_ref[:, sl]
        bonus = (_pair_sum(rr * kd_f, m0) + _pair_sum(rr * kd_b, m0)) * v
        y = yf_ref[:, sl] + yb_ref[:, sl]
        dev = y - _pair_sum(y, m0) * inv_n
        var = _pair_sum(dev * dev, m0) * inv_n
        yn = dev * lax.rsqrt(var + RWKV_GN_EPS) * lw_ref[:, sl] + lb_ref[:, sl]
        act_ref[:, sl] = ((yn + bonus) * gt_ref[:, sl]).astype(BF16)
    o_ref[...] = x_ref[...] + _dot(act_ref[...], wo_ref[...])


def _rwkv_layer(x, g, mu, w_r, w_k, w_v, w0, w1, w2, a0, a1, a2, g1, g2, k_k, k_a, r_k, lnx_w, lnx_b, w_o):
    b, s, d = x.shape
    t = b * s
    tiles = _tiles(t)
    tm = min(tiles["rwkv_proj"], s)
    xf = x.reshape(t, d)
    lora_g = g1.shape[1]
    gpad = -lora_g % LANES
    g1p = jnp.pad(g1, ((0, 0), (0, gpad))).astype(BF16)
    g2p = jnp.pad(g2, ((0, gpad), (0, 0))).astype(BF16)

    def both_dirs(m1, m2):
        l = m1.shape[2]
        z = jnp.zeros((l, d), m2.dtype)
        cat = jnp.concatenate([m1[0], m1[1]], axis=1)
        bd = jnp.concatenate([jnp.concatenate([m2[0], z], axis=1), jnp.concatenate([z, m2[1]], axis=1)], axis=0)
        return cat.astype(BF16), bd.astype(BF16)

    w1c, w2c = both_dirs(w1, w2)
    a1c, a2c = both_dirs(a1, a2)
    full = lambda arr: pl.BlockSpec(arr.shape, lambda i: (0,) * arr.ndim)
    halo = tm // 8
    n8 = t // 8
    tok = pl.BlockSpec((tm, d), lambda i: (i, 0))
    weights = [g.reshape(1, d), mu, w_r.astype(BF16), w_k.astype(BF16), w_v.astype(BF16), g1p, g2p,
               w1c, w2c, w0.reshape(1, 2 * d), a1c, a2c, a0.reshape(1, 2 * d)]
    outs = pl.pallas_call(
        functools.partial(_rwkv_proj_kernel, tiles_per_seq=s // tm),
        grid=(t // tm,),
        in_specs=[tok,
                  pl.BlockSpec((8, d), lambda i: (jnp.maximum(i * halo - 1, 0), 0)),
                  pl.BlockSpec((8, d), lambda i: (jnp.minimum((i + 1) * halo, n8 - 1), 0)),
                  ] + [full(w) for w in weights],
        out_specs=[tok] * 8,
        out_shape=[jax.ShapeDtypeStruct((t, d), F32)] * 8,
        compiler_params=_params("parallel"),
        name="rwkv_proj",
    )(xf, xf, xf, *weights)
    r, k, v, gate, lw_f, lw_b, a_f, a_b = [o.reshape(b, s, d) for o in outs]

    c = RWKV_CHUNK
    nc = s // c
    pairs = 2
    wpair = pairs * LANES
    fwd = pl.BlockSpec((1, c, wpair), lambda bi, p, ci: (bi, ci, p))
    bwd = pl.BlockSpec((1, c, wpair), lambda bi, p, ci: (bi, nc - 1 - ci, p))
    vec = pl.BlockSpec((1, wpair), lambda bi, p, ci: (0, p))
    y_f, y_b = pl.pallas_call(
        functools.partial(_rwkv_scan_kernel, pairs=pairs),
        grid=(b, d // wpair, nc),
        in_specs=[fwd, fwd, fwd, bwd, bwd, bwd, fwd, fwd, bwd, bwd, vec, vec],
        out_specs=[fwd, bwd],
        out_shape=[jax.ShapeDtypeStruct((b, s, d), F32)] * 2,
        scratch_shapes=[pltpu.VMEM((2, pairs, LANES, LANES), F32)],
        compiler_params=_params("parallel", "parallel", "arbitrary"),
        name="rwkv_scan",
    )(r, k, v, r, k, v, lw_f, a_f, lw_b, a_b, k_k.reshape(1, d), k_a.reshape(1, d))

    flat = lambda z: z.reshape(t, d)
    rowv = lambda z: z.reshape(1, d)
    tmo = tiles["rwkv_proj"]
    tok_o = pl.BlockSpec((tmo, d), lambda i: (i, 0))
    vec_o = pl.BlockSpec((1, d), lambda i: (0, 0))
    out = pl.pallas_call(
        _rwkv_out_kernel,
        grid=(t // tmo,),
        in_specs=[tok_o] * 9 + [vec_o] * 4 + [pl.BlockSpec((d, d), lambda i: (0, 0))],
        out_specs=tok_o,
        out_shape=jax.ShapeDtypeStruct((t, d), F32),
        scratch_shapes=[pltpu.VMEM((tmo, d), BF16)],
        compiler_params=_params("parallel"),
        name="rwkv_out",
    )(xf, flat(r), flat(k), flat(v), flat(a_f), flat(a_b), flat(gate), flat(y_f), flat(y_b),
      rowv(k_a), rowv(r_k), rowv(lnx_w), rowv(lnx_b), w_o.astype(BF16))
    return out.reshape(b, s, d)


def kernel(x, norm_tok, norm_ch, ffn_w_up, ffn_conv_w, ffn_conv_b, ffn_w_down, swa_w_qkv, swa_q_gain, swa_k_gain, swa_sinks, swa_w_o, rwkv_mu, rwkv_w_r, rwkv_w_k, rwkv_w_v, rwkv_w0, rwkv_w1, rwkv_w2, rwkv_a0, rwkv_a1, rwkv_a2, rwkv_g1, rwkv_g2, rwkv_k_k, rwkv_k_a, rwkv_r_k, rwkv_lnx_w, rwkv_lnx_b, rwkv_w_o, mla_w_down, mla_cq_gain, mla_ckv_gain, mla_w_uq, mla_w_ukv, mla_q_gain, mla_k_gain, mla_w_o):
    depth = norm_tok.shape[0]
    for i in range(depth):
        kind = i % N_MIXERS
        j = i // N_MIXERS
        if kind == 0:
            x = _swa_layer(x, norm_tok[i], swa_w_qkv[j], swa_q_gain[j], swa_k_gain[j], swa_sinks[j], swa_w_o[j])
        elif kind == 1:
            x = _rwkv_layer(x, norm_tok[i], rwkv_mu[j], rwkv_w_r[j], rwkv_w_k[j], rwkv_w_v[j], rwkv_w0[j],
                            rwkv_w1[j], rwkv_w2[j], rwkv_a0[j], rwkv_a1[j], rwkv_a2[j], rwkv_g1[j], rwkv_g2[j],
                            rwkv_k_k[j], rwkv_k_a[j], rwkv_r_k[j], rwkv_lnx_w[j], rwkv_lnx_b[j], rwkv_w_o[j])
        else:
            x = _mla_layer(x, norm_tok[i], mla_w_down[j], mla_cq_gain[j], mla_ckv_gain[j], mla_w_uq[j],
                           mla_w_ukv[j], mla_q_gain[j], mla_k_gain[j], mla_w_o[j])
        x = _ffn(x, norm_ch[i], ffn_w_up[i], ffn_conv_w[i], ffn_conv_b[i], ffn_w_down[i])
    return x
```

```python
import functools

import jax
import jax.numpy as jnp
from jax import lax
from jax.experimental import pallas as pl
from jax.experimental.pallas import tpu as pltpu

F32 = jnp.float32
BF16 = jnp.bfloat16

N_MIXERS = 3
ROPE_THETA = 500000.0
NORM_EPS = 1e-6
NEG_INF = -1e30
LANES = 128
HEAD_DIM = 64
SWA_HEADS = 16
SWA_KV_HEADS = 4
SWA_WINDOW = 128
SWA_BLOCK = 128
SWA_ROT = HEAD_DIM // 4
RWKV_GN_EPS = 64e-5
RWKV_CHUNK = 64
WKV_PAIRS = 2
WKV_GROUPS_PER_STEP = 4
WKV_SCORE_PASSES = 1
WKV_INV_PASSES = 1
WKV_STATE_PASSES = 3
MLA_HEADS = 16
MLA_NOPE = 64
MLA_ROPE = 32
MLA_V = 64
MLA_Q_LORA = 384
MLA_KV_LORA = 256
MLA_DOWN_PAD = 768
VMEM_LIMIT_BYTES = 56 * 1024 * 1024

NN = ((1,), (0,))
NT = ((1,), (1,))
TN = ((0,), (0,))


def _dot(a, b, dims=NN):
    return lax.dot_general(a, b, (dims, ((), ())), preferred_element_type=F32)


def _split3(x):
    hi = x.astype(BF16)
    r1 = x - hi.astype(F32)
    mid = r1.astype(BF16)
    lo = (r1 - mid.astype(F32)).astype(BF16)
    return hi, mid, lo


def _mm(a, b, dims=NN, passes=1):
    if passes == 1:
        return _dot(a.astype(BF16), b.astype(BF16), dims)
    ah, am, _ = _split3(a)
    bh, bm, _ = _split3(b)
    return _dot(ah, bh, dims) + (_dot(ah, bm, dims) + _dot(am, bh, dims))


def _mm_exact_lhs(a01, b):
    bh, bm, bl = _split3(b)
    a = a01.astype(BF16)
    return _dot(a, bh) + (_dot(a, bm) + _dot(a, bl))


def _rms(x, eps=NORM_EPS):
    return x * lax.rsqrt(jnp.mean(x * x, axis=-1, keepdims=True) + eps)


def _low_half(shape):
    return lax.broadcasted_iota(jnp.int32, shape, 1) < HEAD_DIM


def _pair_sum(x, m0):
    s0 = jnp.sum(jnp.where(m0, x, 0.0), axis=-1, keepdims=True)
    s1 = jnp.sum(jnp.where(m0, 0.0, x), axis=-1, keepdims=True)
    return jnp.where(m0, s0, s1)


def _params(*sem):
    return pltpu.CompilerParams(dimension_semantics=sem, vmem_limit_bytes=VMEM_LIMIT_BYTES)


def _tiles(rows):
    def pick(pref):
        t = pref
        while rows % t:
            t //= 2
        return t
    return {"proj": pick(512), "rwkv_proj": pick(256), "mla_proj": pick(256), "mla_q": pick(256)}


def _ffn_kernel(x_ref, g_ref, wg_ref, wv_ref, cwg_ref, cwv_ref, cbg_ref, cbv_ref, wd_ref, o_ref, hn_ref):
    j = pl.program_id(1)

    @pl.when(j == 0)
    def _():
        x = x_ref[0]
        hn_ref[...] = (_rms(x) * g_ref[...]).astype(BF16)
        o_ref[0] = x

    hn = hn_ref[...]
    s = hn.shape[0]
    row = lax.broadcasted_iota(jnp.int32, (s, wg_ref.shape[1]), 0)

    def conv(u, cw_ref, cb_ref):
        prev = jnp.where(row == 0, 0.0, pltpu.roll(u, 1, 0))
        nxt = jnp.where(row == s - 1, 0.0, pltpu.roll(u, s - 1, 0))
        return prev * cw_ref[0:1, :] + cb_ref[...] + u * cw_ref[1:2, :] + nxt * cw_ref[2:3, :]

    gate = conv(_dot(hn, wg_ref[...]), cwg_ref, cbg_ref)
    val = conv(_dot(hn, wv_ref[...]), cwv_ref, cbv_ref)
    act = (gate * jax.nn.sigmoid(gate) * val).astype(BF16)
    o_ref[0] += _dot(act, wd_ref[...])


def _ffn(x, g, w_up, conv_w, conv_b, w_down, tf=256):
    b, s, d = x.shape
    f = w_down.shape[0]
    nf = f // tf
    wu = w_up.astype(BF16)
    wd = w_down.astype(BF16)
    cb = conv_b.reshape(1, 2 * f)
    return pl.pallas_call(
        _ffn_kernel,
        grid=(b, nf),
        in_specs=[
            pl.BlockSpec((1, s, d), lambda i, j: (i, 0, 0)),
            pl.BlockSpec((1, d), lambda i, j: (0, 0)),
            pl.BlockSpec((d, tf), lambda i, j: (0, j)),
            pl.BlockSpec((d, tf), lambda i, j: (0, nf + j)),
            pl.BlockSpec((3, tf), lambda i, j: (0, j)),
            pl.BlockSpec((3, tf), lambda i, j: (0, nf + j)),
            pl.BlockSpec((1, tf), lambda i, j: (0, j)),
            pl.BlockSpec((1, tf), lambda i, j: (0, nf + j)),
            pl.BlockSpec((tf, d), lambda i, j: (j, 0)),
        ],
        out_specs=pl.BlockSpec((1, s, d), lambda i, j: (i, 0, 0)),
        out_shape=jax.ShapeDtypeStruct((b, s, d), F32),
        scratch_shapes=[pltpu.VMEM((s, d), BF16)],
        compiler_params=_params("parallel", "arbitrary"),
        name="conv_ffn",
    )(x, g.reshape(1, d), wu, wu, conv_w, conv_w, cb, cb, wd)


def _proj_res_kernel(a_ref, w_ref, x_ref, o_ref):
    o_ref[...] = x_ref[...] + _dot(a_ref[...], w_ref[...])


def _proj_res(a, w, x):
    t, k = a.shape
    d = w.shape[1]
    tm = _tiles(t)["proj"]
    return pl.pallas_call(
        _proj_res_kernel,
        grid=(t // tm,),
        in_specs=[
            pl.BlockSpec((tm, k), lambda i: (i, 0)),
            pl.BlockSpec((k, d), lambda i: (0, 0)),
            pl.BlockSpec((tm, d), lambda i: (i, 0)),
        ],
        out_specs=pl.BlockSpec((tm, d), lambda i: (i, 0)),
        out_shape=jax.ShapeDtypeStruct((t, d), F32),
        compiler_params=_params("parallel"),
        name="proj_residual",
    )(a, w.astype(BF16), x)


def _rope_apply(xn, cos_t, sin_up, sin_dn, half):
    return xn * cos_t + pltpu.roll(xn, LANES - half, 1) * sin_up + pltpu.roll(xn, half, 1) * sin_dn


def _rope_tables(s, rot, start, group):
    half = rot // 2
    inv = ROPE_THETA ** (-jnp.arange(0, rot, 2, dtype=F32) / rot)
    ang = jnp.arange(s, dtype=F32)[:, None] * inv[None, :]
    cos, sin = jnp.cos(ang), jnp.sin(ang)
    lane = jnp.arange(LANES) % group
    first = (lane >= start) & (lane < start + half)
    second = (lane >= start + half) & (lane < start + rot)
    idx = jnp.clip(jnp.where(second, lane - start - half, lane - start), 0, half - 1)
    cos_t = jnp.where(first | second, cos[:, idx], 1.0)
    sin_up = jnp.where(first, -sin[:, idx], 0.0)
    sin_dn = jnp.where(second, sin[:, idx], 0.0)
    return cos_t, sin_up, sin_dn


def _swa_qkv_kernel(x_ref, g_ref, w_ref, qg_ref, kg_ref, c_ref, su_ref, sd_ref, q_ref, k_ref, v_ref):
    hn = (_rms(x_ref[...]) * g_ref[...]).astype(BF16)
    y = _dot(hn, w_ref[...])
    tm = y.shape[0]
    nq = q_ref.shape[1]
    nk = k_ref.shape[1]
    m0 = _low_half((tm, LANES))
    cos_t, sin_up, sin_dn = c_ref[...], su_ref[...], sd_ref[...]

    def head_norm_rope(xc, gain):
        ss = _pair_sum(xc * xc, m0)
        xn = xc * lax.rsqrt(ss * (1.0 / HEAD_DIM) + NORM_EPS) * gain
        return _rope_apply(xn, cos_t, sin_up, sin_dn, SWA_ROT // 2)

    for c in range(nq // LANES):
        sl = slice(c * LANES, (c + 1) * LANES)
        q_ref[:, sl] = head_norm_rope(y[:, sl], qg_ref[...]).astype(BF16)
    for c in range(nk // LANES):
        sl = slice(c * LANES, (c + 1) * LANES)
        k_ref[:, sl] = head_norm_rope(y[:, nq + c * LANES:nq + (c + 1) * LANES], kg_ref[...]).astype(BF16)
    v_ref[...] = y[:, nq + nk:].astype(BF16)


def _swa_attn_kernel(sink_ref, q_ref, k_ref, v_ref, o_ref):
    i = pl.program_id(1)
    s = k_ref.shape[1]
    blk = q_ref.shape[1]
    band = 3 * blk
    start = pl.multiple_of(jnp.clip((i - 1) * blk, 0, s - band), blk)
    group = SWA_HEADS // SWA_KV_HEADS
    rows = group * blk
    qpos = i * blk + lax.broadcasted_iota(jnp.int32, (rows, band), 0) % blk
    kpos = start + lax.broadcasted_iota(jnp.int32, (rows, band), 1)
    mask = jnp.abs(qpos - kpos) <= SWA_WINDOW
    m0q = _low_half((blk, LANES))
    rowblk = lax.broadcasted_iota(jnp.int32, (rows, 1), 0) // blk
    zq = jnp.zeros((blk, LANES), BF16)
    for g in range(SWA_KV_HEADS):
        parts = []
        for p in range(group // 2):
            qp = q_ref[0, :, (g * (group // 2) + p) * LANES:(g * (group // 2) + p + 1) * LANES]
            parts += [jnp.where(m0q, qp, zq), jnp.where(m0q, zq, qp)]
        qs = jnp.concatenate(parts, axis=0)
        kb = k_ref[0, pl.ds(start, band), g * LANES:(g + 1) * LANES]
        vb = v_ref[0, pl.ds(start, band), g * LANES:(g + 1) * LANES]
        sc = jnp.where(mask, _dot(qs, kb, NT), NEG_INF)
        sink = jnp.zeros((rows, 1), F32)
        for jh in range(group):
            sink = jnp.where(rowblk == jh, sink_ref[g * group + jh], sink)
        mx = jnp.maximum(jnp.max(sc, axis=-1, keepdims=True), sink)
        pe = jnp.exp(sc - mx)
        den = jnp.sum(pe, axis=-1, keepdims=True) + jnp.exp(sink - mx)
        pn = (pe / den).astype(BF16)
        ob = _dot(pn, vb)
        for p in range(group // 2):
            o0 = ob[(2 * p) * blk:(2 * p + 1) * blk]
            o1 = ob[(2 * p + 1) * blk:(2 * p + 2) * blk]
            col = (g * (group // 2) + p) * LANES
            o_ref[0, :, col:col + LANES] = jnp.where(m0q, o0, o1).astype(BF16)


def _swa_layer(x, g, w_qkv, q_gain, k_gain, sinks, w_o):
    b, s, d = x.shape
    t = b * s
    qd = SWA_HEADS * HEAD_DIM
    kd = SWA_KV_HEADS * HEAD_DIM
    wq = w_qkv[:, :qd]
    wk = w_qkv[:, qd:qd + kd].reshape(d, SWA_KV_HEADS, 1, HEAD_DIM)
    wv = w_qkv[:, qd + kd:].reshape(d, SWA_KV_HEADS, 1, HEAD_DIM)
    dup = lambda w: jnp.broadcast_to(w, (d, SWA_KV_HEADS, 2, HEAD_DIM)).reshape(d, 2 * kd)
    w_all = jnp.concatenate([wq, dup(wk), dup(wv)], axis=1).astype(BF16)
    nk = 2 * kd
    qg = (jnp.tile(q_gain, 2) * HEAD_DIM ** -0.5).reshape(1, LANES)
    kg = jnp.tile(k_gain, 2).reshape(1, LANES)
    cos_t, sin_up, sin_dn = _rope_tables(s, SWA_ROT, 0, HEAD_DIM)
    tm = _tiles(t)["proj"]
    tm = min(tm, s)
    nt = s // tm
    tab = pl.BlockSpec((tm, LANES), lambda i: (i % nt, 0))
    q, k, v = pl.pallas_call(
        _swa_qkv_kernel,
        grid=(t // tm,),
        in_specs=[
            pl.BlockSpec((tm, d), lambda i: (i, 0)),
            pl.BlockSpec((1, d), lambda i: (0, 0)),
            pl.BlockSpec((d, qd + 2 * nk), lambda i: (0, 0)),
            pl.BlockSpec((1, LANES), lambda i: (0, 0)),
            pl.BlockSpec((1, LANES), lambda i: (0, 0)),
            tab, tab, tab,
        ],
        out_specs=[
            pl.BlockSpec((tm, qd), lambda i: (i, 0)),
            pl.BlockSpec((tm, nk), lambda i: (i, 0)),
            pl.BlockSpec((tm, nk), lambda i: (i, 0)),
        ],
        out_shape=[jax.ShapeDtypeStruct((t, qd), BF16), jax.ShapeDtypeStruct((t, nk), BF16),
                   jax.ShapeDtypeStruct((t, nk), BF16)],
        compiler_params=_params("parallel"),
        name="swa_qkv",
    )(x.reshape(t, d), g.reshape(1, d), w_all, qg, kg, cos_t, sin_up, sin_dn)

    nb = s // SWA_BLOCK
    o = pl.pallas_call(
        _swa_attn_kernel,
        grid_spec=pltpu.PrefetchScalarGridSpec(
            num_scalar_prefetch=1,
            grid=(b, nb),
            in_specs=[
                pl.BlockSpec((1, SWA_BLOCK, qd), lambda bi, i, sk: (bi, i, 0)),
                pl.BlockSpec((1, s, nk), lambda bi, i, sk: (bi, 0, 0)),
                pl.BlockSpec((1, s, nk), lambda bi, i, sk: (bi, 0, 0)),
            ],
            out_specs=pl.BlockSpec((1, SWA_BLOCK, qd), lambda bi, i, sk: (bi, i, 0)),
        ),
        out_shape=jax.ShapeDtypeStruct((b, s, qd), BF16),
        compiler_params=_params("parallel", "arbitrary"),
        name="swa_attention",
    )(sinks.astype(F32), q.reshape(b, s, qd), k.reshape(b, s, nk), v.reshape(b, s, nk))
    return _proj_res(o.reshape(t, qd), w_o, x.reshape(t, d)).reshape(b, s, d)


def _mla_proj_kernel(x_ref, g_ref, wd_ref, cqg_ref, ckvg_ref, wuq_ref, wuk_ref, wuv_ref, qg_ref, kg_ref,
                     c_ref, su_ref, sd_ref, q_ref, k_ref, v_ref):
    hn = (_rms(x_ref[...]) * g_ref[...]).astype(BF16)
    down = _dot(hn, wd_ref[...])
    cq = (_rms(down[:, :MLA_Q_LORA]) * cqg_ref[...]).astype(BF16)
    ckv = (_rms(down[:, MLA_Q_LORA:MLA_Q_LORA + MLA_KV_LORA]) * ckvg_ref[...]).astype(BF16)
    tail = down[:, MLA_Q_LORA + MLA_KV_LORA:]
    k_rope = pltpu.roll(tail, MLA_NOPE, 1)
    qp = _dot(cq, wuq_ref[...])
    kp = _dot(ckv, wuk_ref[...])
    v_ref[...] = _dot(ckv, wuv_ref[...]).astype(BF16)
    cos_t, sin_up, sin_dn = c_ref[...], su_ref[...], sd_ref[...]
    inv_dim = 1.0 / (MLA_NOPE + MLA_ROPE)

    def head_norm_rope(xh, gain):
        ss = jnp.sum(xh * xh, axis=-1, keepdims=True)
        xn = xh * lax.rsqrt(ss * inv_dim + NORM_EPS) * gain
        return _rope_apply(xn, cos_t, sin_up, sin_dn, MLA_ROPE // 2)

    for h in range(MLA_HEADS):
        sl = slice(h * LANES, (h + 1) * LANES)
        q_ref[:, sl] = head_norm_rope(qp[:, sl], qg_ref[...]).astype(BF16)
        k_ref[:, sl] = head_norm_rope(kp[:, sl] + k_rope, kg_ref[...]).astype(BF16)


def _mla_attn_kernel(q_ref, k_ref, v_ref, o_ref):
    vv = v_ref[0]
    outs = []
    for m in range(2):
        sl = slice(m * LANES, (m + 1) * LANES)
        sc = _dot(q_ref[0, :, sl], k_ref[0, :, sl], NT)
        pe = jnp.exp(sc - jnp.max(sc, axis=-1, keepdims=True))
        den = jnp.sum(pe, axis=-1, keepdims=True)
        outs.append(_dot(pe.astype(BF16), vv) / den)
    o_ref[0] = jnp.where(_low_half(outs[0].shape), outs[0], outs[1]).astype(BF16)


def _mla_layer(x, g, w_down, cq_gain, ckv_gain, w_uq, w_ukv, q_gain, k_gain, w_o):
    b, s, d = x.shape
    t = b * s
    hd = MLA_NOPE + MLA_ROPE
    n_down = w_down.shape[1]
    wd = jnp.pad(w_down, ((0, 0), (0, MLA_DOWN_PAD - n_down))).astype(BF16)
    pad_heads = lambda w, width: jnp.pad(w, ((0, 0), (0, 0), (0, LANES - width))).reshape(w.shape[0], MLA_HEADS * LANES)
    wuq = pad_heads(w_uq.reshape(MLA_Q_LORA, MLA_HEADS, hd), hd).astype(BF16)
    wkv = w_ukv.reshape(MLA_KV_LORA, MLA_HEADS, MLA_NOPE + MLA_V)
    wuk = pad_heads(wkv[:, :, :MLA_NOPE], MLA_NOPE).astype(BF16)
    wuv = wkv[:, :, MLA_NOPE:].reshape(MLA_KV_LORA, MLA_HEADS * MLA_V).astype(BF16)
    qg = jnp.pad(q_gain * hd ** -0.5, (0, LANES - hd)).reshape(1, LANES)
    kg = jnp.pad(k_gain, (0, LANES - hd)).reshape(1, LANES)
    cos_t, sin_up, sin_dn = _rope_tables(s, MLA_ROPE, MLA_NOPE, LANES)
    tiles = _tiles(t)
    tm = min(tiles["mla_proj"], s)
    nt = s // tm
    nq = MLA_HEADS * LANES
    nv = MLA_HEADS * MLA_V
    full = lambda shape: pl.BlockSpec(shape, lambda i: (0, 0))
    tab = pl.BlockSpec((tm, LANES), lambda i: (i % nt, 0))
    q, k, v = pl.pallas_call(
        _mla_proj_kernel,
        grid=(t // tm,),
        in_specs=[
            pl.BlockSpec((tm, d), lambda i: (i, 0)),
            full((1, d)), full((d, MLA_DOWN_PAD)), full((1, MLA_Q_LORA)), full((1, MLA_KV_LORA)),
            full((MLA_Q_LORA, nq)), full((MLA_KV_LORA, nq)), full((MLA_KV_LORA, nv)),
            full((1, LANES)), full((1, LANES)), tab, tab, tab,
        ],
        out_specs=[
            pl.BlockSpec((tm, nq), lambda i: (i, 0)),
            pl.BlockSpec((tm, nq), lambda i: (i, 0)),
            pl.BlockSpec((tm, nv), lambda i: (i, 0)),
        ],
        out_shape=[jax.ShapeDtypeStruct((t, nq), BF16), jax.ShapeDtypeStruct((t, nq), BF16),
                   jax.ShapeDtypeStruct((t, nv), BF16)],
        compiler_params=_params("parallel"),
        name="mla_proj",
    )(x.reshape(t, d), g.reshape(1, d), wd, cq_gain.reshape(1, -1), ckv_gain.reshape(1, -1), wuq, wuk, wuv,
      qg, kg, cos_t, sin_up, sin_dn)

    tq = min(tiles["mla_q"], s)
    o = pl.pallas_call(
        _mla_attn_kernel,
        grid=(b, MLA_HEADS // 2, s // tq),
        in_specs=[
            pl.BlockSpec((1, tq, 2 * LANES), lambda bi, p, i: (bi, i, p)),
            pl.BlockSpec((1, s, 2 * LANES), lambda bi, p, i: (bi, 0, p)),
            pl.BlockSpec((1, s, LANES), lambda bi, p, i: (bi, 0, p)),
        ],
        out_specs=pl.BlockSpec((1, tq, LANES), lambda bi, p, i: (bi, i, p)),
        out_shape=jax.ShapeDtypeStruct((b, s, nv), BF16),
        compiler_params=_params("parallel", "parallel", "arbitrary"),
        name="mla_attention",
    )(q.reshape(b, s, nq), k.reshape(b, s, nq), v.reshape(b, s, nv))
    return _proj_res(o.reshape(t, nv), w_o, x.reshape(t, d)).reshape(b, s, d)


def _rwkv_proj_kernel(x_ref, xp_ref, xn_ref, g_ref, mu_ref, wr_ref, wk_ref, wv_ref, g1_ref, g2_ref,
                      w1_ref, w2_ref, w0_ref, a1_ref, a2_ref, a0_ref,
                      r_ref, k_ref, v_ref, gt_ref, lwf_ref, lwb_ref, af_ref, ab_ref, *, tiles_per_seq):
    i = pl.program_id(0)
    gain = g_ref[...]
    h = _rms(x_ref[...]) * gain
    tm = h.shape[0]
    first = (i % tiles_per_seq) == 0
    last = (i % tiles_per_seq) == tiles_per_seq - 1
    h_before = jnp.where(first, 0.0, _rms(xp_ref[7:8, :]) * gain)
    h_after = jnp.where(last, 0.0, _rms(xn_ref[0:1, :]) * gain)
    row = lax.broadcasted_iota(jnp.int32, h.shape, 0)
    h_prev = jnp.where(row == 0, h_before, pltpu.roll(h, 1, 0))
    h_next = jnp.where(row == tm - 1, h_after, pltpu.roll(h, tm - 1, 0))
    xx = 0.5 * (h_prev + h_next) - h
    mix = lambda n: (h + xx * mu_ref[n:n + 1, :]).astype(BF16)
    r_ref[...] = _dot(mix(0), wr_ref[...])
    k_ref[...] = _dot(mix(2), wk_ref[...])
    v_ref[...] = _dot(mix(3), wv_ref[...])
    gt_ref[...] = _dot(jax.nn.sigmoid(_dot(mix(5), g1_ref[...])).astype(BF16), g2_ref[...])
    d = h.shape[1]
    wl = w0_ref[...] + _dot(jnp.tanh(_dot(mix(1), w1_ref[...])).astype(BF16), w2_ref[...])
    lw = -jnp.exp(-0.5) * jax.nn.sigmoid(wl)
    lwf_ref[...] = lw[:, :d]
    lwb_ref[...] = lw[:, d:]
    al = jax.nn.sigmoid(a0_ref[...] + _dot(_dot(mix(4), a1_ref[...]).astype(BF16), a2_ref[...]))
    af_ref[...] = al[:, :d]
    ab_ref[...] = al[:, d:]


def _wkv_masks(c, rev):
    n = 2 * WKV_PAIRS * c
    r = lax.broadcasted_iota(jnp.int32, (n, WKV_PAIRS * c), 0)
    q = lax.broadcasted_iota(jnp.int32, (n, WKV_PAIRS * c), 1)
    i, j = r % c, q % c
    same_pair = ((r // c) % WKV_PAIRS) == (q // c)
    upto = same_pair & ((j >= i) if rev else (j <= i))
    before = same_pair & ((j > i) if rev else (j < i))
    half = r // (WKV_PAIRS * c)
    rn = lax.broadcasted_iota(jnp.int32, (n, n), 0)
    cn = lax.broadcasted_iota(jnp.int32, (n, n), 1)
    ns = WKV_PAIRS * LANES
    r2 = lax.broadcasted_iota(jnp.int32, (ns, ns), 0)
    c2 = lax.broadcasted_iota(jnp.int32, (ns, ns), 1)
    ri = lax.broadcasted_iota(jnp.int32, (c, c), 0)
    ci = lax.broadcasted_iota(jnp.int32, (c, c), 1)
    sizes = []
    s = 2
    while s < c:
        sizes.append(s)
        s *= 2
    return {
        "upto": [upto & (half == m) for m in range(2)],
        "before": [before & (half == m) for m in range(2)],
        "join": [((rn // (2 * s)) == (cn // (2 * s))) & ((rn // s) != (cn // s)) for s in sizes],
        "base": (rn // 2) == (cn // 2),
        "eye": jnp.where(rn == cn, 1.0, 0.0),
        "diag": r2 == c2,
        "same_head": (r2 // HEAD_DIM) == (c2 // HEAD_DIM),
        "cum": jnp.where((ci >= ri) if rev else (ci <= ri), 1.0, 0.0),
    }


def _cat_l(parts):
    return jnp.concatenate(parts, axis=1)


def _cat_r(parts):
    return jnp.concatenate(parts, axis=0)


def _pair(t, p):
    return t[:, p * LANES:(p + 1) * LANES]


def _stack_pairs(t, reps):
    return _cat_r([_pair(t, p) for _ in range(reps) for p in range(WKV_PAIRS)])


def _wkv_prepare(r, lw, k, v, asig, k_k, k_a, rev, masks):
    c = r.shape[0]
    m0p = _low_half((c, LANES))
    pairs = range(WKV_PAIRS)
    pair, cat_l, cat_r = _pair, _cat_l, _cat_r

    kk = k * k_k
    kk2 = kk * kk
    kk = kk / jnp.maximum(jnp.sqrt(cat_l([_pair_sum(pair(kk2, p), m0p) for p in pairs])), 1e-12)
    bvec = kk * asig
    kd = k * (1.0 + (asig - 1.0) * k_a)

    cl = _mm_exact_lhs(masks["cum"], lw)
    tot = cl[0:1, :] if rev else cl[c - 1:c, :]
    g_inv = jnp.exp(-cl)
    g_end = jnp.exp(tot - cl)
    at = -kk * jnp.exp(cl - lw)
    rt = r * jnp.exp(cl)
    bt = bvec * g_inv
    kt = kd * g_inv
    b_end = bvec * g_end
    k_end = kd * g_end

    lane = lax.broadcasted_iota(jnp.int32, r.shape, 1)
    low = lane % LANES < HEAD_DIM
    lhs = cat_r([jnp.where(low, at, 0.0), jnp.where(low, 0.0, at), jnp.where(low, rt, 0.0), jnp.where(low, 0.0, rt)])
    rhs = cat_r([jnp.where(lane // LANES == p, t, 0.0) for t in (bt, kt) for p in pairs])
    sc = _mm(lhs, rhs, NT, WKV_SCORE_PASSES)
    pc = WKV_PAIRS * c

    def block_diag(lhs_kind, rhs_kind, key):
        s = sc[lhs_kind * 2 * c:(lhs_kind + 1) * 2 * c, rhs_kind * pc:(rhs_kind + 1) * pc]
        st = cat_r([s[m * c:(m + 1) * c] for m in range(2) for _ in pairs])
        return cat_l([jnp.where(masks[key][m], st, 0.0) for m in range(2)])

    return {"a_ab": block_diag(0, 0, "before"), "a_rb": block_diag(1, 0, "upto"),
            "a_ak": block_diag(0, 1, "before"), "a_rk": block_diag(1, 1, "upto"),
            "at": at, "rt": rt, "v": v, "b_end": b_end, "k_end": k_end, "decay": jnp.exp(tot)}


def _wkv_chunks(units):
    pres = [u[0] for u in units]
    states = [u[1] for u in units]
    masks = [u[2] for u in units]
    c = pres[0]["at"].shape[0]
    m0p = _low_half((c, LANES))
    inv = [mk["eye"] + jnp.where(mk["base"], p["a_ab"], 0.0) for p, mk in zip(pres, masks)]
    for lvl in range(len(masks[0]["join"])):
        ex = [_mm(jnp.where(mk["join"][lvl], p["a_ab"], 0.0), x, NN, WKV_INV_PASSES)
              for p, mk, x in zip(pres, masks, inv)]
        inv = [x + _mm(x, t, NN, WKV_INV_PASSES) for x, t in zip(inv, ex)]
    v_st = [_stack_pairs(p["v"], 2) for p in pres]
    x_ak = [_mm(p["a_ak"], vs) for p, vs in zip(pres, v_st)]
    yk = [_mm(p["a_rk"], vs) for p, vs in zip(pres, v_st)]
    zz = [_mm(x, _cat_l([_stack_pairs(p["at"], 2), xa]), NN, WKV_INV_PASSES)
          for x, p, xa in zip(inv, pres, x_ak)]

    def unstack(t):
        return _cat_l([jnp.where(m0p, t[p * c:(p + 1) * c], t[(WKV_PAIRS + p) * c:(WKV_PAIRS + p + 1) * c])
                       for p in range(WKV_PAIRS)])

    w_a = [unstack(z[:, :LANES]) for z in zz]
    u_0 = [unstack(z[:, LANES:]) for z in zz]
    hw = [_mm(_cat_r([w, p["rt"]]), st, NN, WKV_STATE_PASSES) for w, p, st in zip(w_a, pres, states)]
    u = [h[:c] + u0 for h, u0 in zip(hw, u_0)]
    yu = [_mm(p["a_rb"], _stack_pairs(uu, 2)) for p, uu in zip(pres, u)]
    grow = [_mm(_cat_r([p["b_end"], p["k_end"]]), _cat_r([uu, p["v"]]), TN) for p, uu in zip(pres, u)]
    out = []
    for p, mk, st, h, yuu, ykk, gr in zip(pres, masks, states, hw, yu, yk, grow):
        y = h[c:] + unstack(yuu + ykk)
        decay_col = jnp.sum(jnp.where(mk["diag"], p["decay"], 0.0), axis=-1, keepdims=True)
        out.append((y, st * decay_col + jnp.where(mk["same_head"], gr, 0.0)))
    return out


def _rwkv_scan_kernel(rf_ref, kf_ref, vf_ref, rb_ref, kb_ref, vb_ref, lwf_ref, af_ref, lwb_ref, ab_ref,
                      kk_ref, ka_ref, yf_ref, yb_ref, st_ref, *, groups):
    @pl.when(pl.program_id(2) == 0)
    def _():
        st_ref[...] = jnp.zeros_like(st_ref)

    c = rf_ref.shape[1]
    width = WKV_PAIRS * LANES
    dirs = ((rf_ref, kf_ref, vf_ref, lwf_ref, af_ref, False), (rb_ref, kb_ref, vb_ref, lwb_ref, ab_ref, True))
    units = []
    for di, (r_ref, k_ref, v_ref, lw_ref, a_ref, rev) in enumerate(dirs):
        masks = _wkv_masks(c, rev)
        for gi in range(groups):
            sl = slice(gi * width, (gi + 1) * width)
            pre = _wkv_prepare(r_ref[0, :, sl], lw_ref[0, :, sl], k_ref[0, :, sl], v_ref[0, :, sl],
                               a_ref[0, :, sl], kk_ref[:, sl], ka_ref[:, sl], rev, masks)
            units.append((pre, st_ref[di, gi], masks))
    results = _wkv_chunks(units)
    for ui, (y, st) in enumerate(results):
        di, gi = divmod(ui, groups)
        (yf_ref, yb_ref)[di][0, :, gi * width:(gi + 1) * width] = y
        st_ref[di, gi] = st


def _rwkv_out_kernel(x_ref, r_ref, k_ref, v_ref, af_ref, ab_ref, gt_ref, yf_ref, yb_ref,
                     ka_ref, rk_ref, lw_ref, lb_ref, wo_ref, o_ref, act_ref):
    tm = x_ref.shape[0]
    m0 = _low_half((tm, LANES))
    inv_n = 1.0 / HEAD_DIM
    for c in range(x_ref.shape[1] // LANES):
        sl = slice(c * LANES, (c + 1) * LANES)
        r, k, v = r_ref[:, sl], k_ref[:, sl], v_ref[:, sl]
        k_a = ka_ref[:, sl]
        kd_f = k * (1.0 + (af_ref[:, sl] - 1.0) * k_a)
        kd_b = k * (1.0 + (ab_ref[:, sl] - 1.0) * k_a)
        rr = r * rk_ref[:, sl]
        bonus = (_pair_sum(rr * kd_f, m0) + _pair_sum(rr * kd_b, m0)) * v
        y = yf_ref[:, sl] + yb_ref[:, sl]
        dev = y - _pair_sum(y, m0) * inv_n
        var = _pair_sum(dev * dev, m0) * inv_n
        yn = dev * lax.rsqrt(var + RWKV_GN_EPS) * lw_ref[:, sl] + lb_ref[:, sl]
        act_ref[:, sl] = ((yn + bonus) * gt_ref[:, sl]).astype(BF16)
    o_ref[...] = x_ref[...] + _dot(act_ref[...], wo_ref[...])


def _rwkv_layer(x, g, mu, w_r, w_k, w_v, w0, w1, w2, a0, a1, a2, g1, g2, k_k, k_a, r_k, lnx_w, lnx_b, w_o):
    b, s, d = x.shape
    t = b * s
    tiles = _tiles(t)
    tm = min(tiles["rwkv_proj"], s)
    xf = x.reshape(t, d)
    lora_g = g1.shape[1]
    gpad = -lora_g % LANES
    g1p = jnp.pad(g1, ((0, 0), (0, gpad))).astype(BF16)
    g2p = jnp.pad(g2, ((0, gpad), (0, 0))).astype(BF16)

    def both_dirs(m1, m2):
        l = m1.shape[2]
        z = jnp.zeros((l, d), m2.dtype)
        cat = jnp.concatenate([m1[0], m1[1]], axis=1)
        bd = jnp.concatenate([jnp.concatenate([m2[0], z], axis=1), jnp.concatenate([z, m2[1]], axis=1)], axis=0)
        return cat.astype(BF16), bd.astype(BF16)

    w1c, w2c = both_dirs(w1, w2)
    a1c, a2c = both_dirs(a1, a2)
    full = lambda arr: pl.BlockSpec(arr.shape, lambda i: (0,) * arr.ndim)
    halo = tm // 8
    n8 = t // 8
    tok = pl.BlockSpec((tm, d), lambda i: (i, 0))
    weights = [g.reshape(1, d), mu, w_r.astype(BF16), w_k.astype(BF16), w_v.astype(BF16), g1p, g2p,
               w1c, w2c, w0.reshape(1, 2 * d), a1c, a2c, a0.reshape(1, 2 * d)]
    outs = pl.pallas_call(
        functools.partial(_rwkv_proj_kernel, tiles_per_seq=s // tm),
        grid=(t // tm,),
        in_specs=[tok,
                  pl.BlockSpec((8, d), lambda i: (jnp.maximum(i * halo - 1, 0), 0)),
                  pl.BlockSpec((8, d), lambda i: (jnp.minimum((i + 1) * halo, n8 - 1), 0)),
                  ] + [full(w) for w in weights],
        out_specs=[tok] * 8,
        out_shape=[jax.ShapeDtypeStruct((t, d), F32)] * 8,
        compiler_params=_params("parallel"),
        name="rwkv_proj",
    )(xf, xf, xf, *weights)
    r, k, v, gate, lw_f, lw_b, a_f, a_b = [o.reshape(b, s, d) for o in outs]

    c = RWKV_CHUNK
    nc = s // c
    groups = WKV_GROUPS_PER_STEP
    wpair = groups * WKV_PAIRS * LANES
    n_state = 2 * WKV_PAIRS * HEAD_DIM
    fwd = pl.BlockSpec((1, c, wpair), lambda bi, p, ci: (bi, ci, p))
    bwd = pl.BlockSpec((1, c, wpair), lambda bi, p, ci: (bi, nc - 1 - ci, p))
    vec = pl.BlockSpec((1, wpair), lambda bi, p, ci: (0, p))
    y_f, y_b = pl.pallas_call(
        functools.partial(_rwkv_scan_kernel, groups=groups),
        grid=(b, d // wpair, nc),
        in_specs=[fwd, fwd, fwd, bwd, bwd, bwd, fwd, fwd, bwd, bwd, vec, vec],
        out_specs=[fwd, bwd],
        out_shape=[jax.ShapeDtypeStruct((b, s, d), F32)] * 2,
        scratch_shapes=[pltpu.VMEM((2, groups, n_state, n_state), F32)],
        compiler_params=_params("parallel", "parallel", "arbitrary"),
        name="rwkv_scan",
    )(r, k, v, r, k, v, lw_f, a_f, lw_b, a_b, k_k.reshape(1, d), k_a.reshape(1, d))

    flat = lambda z: z.reshape(t, d)
    rowv = lambda z: z.reshape(1, d)
    tmo = tiles["rwkv_proj"]
    tok_o = pl.BlockSpec((tmo, d), lambda i: (i, 0))
    vec_o = pl.BlockSpec((1, d), lambda i: (0, 0))
    out = pl.pallas_call(
        _rwkv_out_kernel,
        grid=(t // tmo,),
        in_specs=[tok_o] * 9 + [vec_o] * 4 + [pl.BlockSpec((d, d), lambda i: (0, 0))],
        out_specs=tok_o,
        out_shape=jax.ShapeDtypeStruct((t, d), F32),
        scratch_shapes=[pltpu.VMEM((tmo, d), BF16)],
        compiler_params=_params("parallel"),
        name="rwkv_out",
    )(xf, flat(r), flat(k), flat(v), flat(a_f), flat(a_b), flat(gate), flat(y_f), flat(y_b),
      rowv(k_a), rowv(r_k), rowv(lnx_w), rowv(lnx_b), w_o.astype(BF16))
    return out.reshape(b, s, d)


def kernel(x, norm_tok, norm_ch, ffn_w_up, ffn_conv_w, ffn_conv_b, ffn_w_down, swa_w_qkv, swa_q_gain, swa_k_gain, swa_sinks, swa_w_o, rwkv_mu, rwkv_w_r, rwkv_w_k, rwkv_w_v, rwkv_w0, rwkv_w1, rwkv_w2, rwkv_a0, rwkv_a1, rwkv_a2, rwkv_g1, rwkv_g2, rwkv_k_k, rwkv_k_a, rwkv_r_k, rwkv_lnx_w, rwkv_lnx_b, rwkv_w_o, mla_w_down, mla_cq_gain, mla_ckv_gain, mla_w_uq, mla_w_ukv, mla_q_gain, mla_k_gain, mla_w_o):
    depth = norm_tok.shape[0]
    for i in range(depth):
        kind = i % N_MIXERS
        j = i // N_MIXERS
        if kind == 0:
            x = _swa_layer(x, norm_tok[i], swa_w_qkv[j], swa_q_gain[j], swa_k_gain[j], swa_sinks[j], swa_w_o[j])
        elif kind == 1:
            x = _rwkv_layer(x, norm_tok[i], rwkv_mu[j], rwkv_w_r[j], rwkv_w_k[j], rwkv_w_v[j], rwkv_w0[j],
                            rwkv_w1[j], rwkv_w2[j], rwkv_a0[j], rwkv_a1[j], rwkv_a2[j], rwkv_g1[j], rwkv_g2[j],
                            rwkv_k_k[j], rwkv_k_a[j], rwkv_r_k[j], rwkv_lnx_w[j], rwkv_lnx_b[j], rwkv_w_o[j])
        else:
            x = _mla_layer(x, norm_tok[i], mla_w_down[j], mla_cq_gain[j], mla_ckv_gain[j], mla_w_uq[j],
                           mla_w_ukv[j], mla_q_gain[j], mla_k_gain[j], mla_w_o[j])
        x = _ffn(x, norm_ch[i], ffn_w_up[i], ffn_conv_w[i], ffn_conv_b[i], ffn_w_down[i])
    return x
```

```python
import functools

import jax
import jax.numpy as jnp
from jax import lax
from jax.experimental import pallas as pl
from jax.experimental.pallas import tpu as pltpu

F32 = jnp.float32
BF16 = jnp.bfloat16

N_MIXERS = 3
ROPE_THETA = 500000.0
NORM_EPS = 1e-6
NEG_INF = -1e30
LANES = 128
HEAD_DIM = 64
SWA_HEADS = 16
SWA_KV_HEADS = 4
SWA_WINDOW = 128
SWA_BLOCK = 128
SWA_ROT = HEAD_DIM // 4
FFN_ROW_CHUNKS = 8
RWKV_GN_EPS = 64e-5
RWKV_CHUNK = 64
WKV_PAIRS = 2
WKV_GROUPS_PER_STEP = 4
WKV_SCORE_PASSES = 1
WKV_INV_PASSES = 1
WKV_STATE_PASSES = 3
MLA_HEADS = 16
MLA_NOPE = 64
MLA_ROPE = 32
MLA_V = 64
MLA_Q_LORA = 384
MLA_KV_LORA = 256
MLA_DOWN_PAD = 768
MLA_KEY_CHUNKS = 4
VMEM_LIMIT_BYTES = 56 * 1024 * 1024

NN = ((1,), (0,))
NT = ((1,), (1,))
TN = ((0,), (0,))


def _dot(a, b, dims=NN):
    return lax.dot_general(a, b, (dims, ((), ())), preferred_element_type=F32)


def _split3(x):
    hi = x.astype(BF16)
    r1 = x - hi.astype(F32)
    mid = r1.astype(BF16)
    lo = (r1 - mid.astype(F32)).astype(BF16)
    return hi, mid, lo


def _mm(a, b, dims=NN, passes=1):
    if passes == 1:
        return _dot(a.astype(BF16), b.astype(BF16), dims)
    ah, am, _ = _split3(a)
    bh, bm, _ = _split3(b)
    return _dot(ah, bh, dims) + (_dot(ah, bm, dims) + _dot(am, bh, dims))


def _mm_exact_lhs(a01, b):
    bh, bm, bl = _split3(b)
    a = a01.astype(BF16)
    return _dot(a, bh) + (_dot(a, bm) + _dot(a, bl))


def _rms(x, eps=NORM_EPS):
    return x * lax.rsqrt(jnp.mean(x * x, axis=-1, keepdims=True) + eps)


def _low_half(shape):
    return lax.broadcasted_iota(jnp.int32, shape, 1) < HEAD_DIM


def _pair_sum(x, m0):
    s0 = jnp.sum(jnp.where(m0, x, 0.0), axis=-1, keepdims=True)
    s1 = jnp.sum(jnp.where(m0, 0.0, x), axis=-1, keepdims=True)
    return jnp.where(m0, s0, s1)


def _params(*sem):
    return pltpu.CompilerParams(dimension_semantics=sem, vmem_limit_bytes=VMEM_LIMIT_BYTES)


def _tiles(rows):
    def pick(pref):
        t = pref
        while rows % t:
            t //= 2
        return t
    return {"proj": pick(512), "rwkv_proj": pick(256), "mla_proj": pick(256), "mla_q": pick(256)}


def _ffn_kernel(x_ref, g_ref, wg_ref, wv_ref, cwg_ref, cwv_ref, cbg_ref, cbv_ref, wd_ref, o_ref, hn_ref):
    j = pl.program_id(1)

    @pl.when(j == 0)
    def _():
        x = x_ref[0]
        hn_ref[...] = (_rms(x) * g_ref[...]).astype(BF16)
        o_ref[0] = x

    s = hn_ref.shape[0]
    rc = s // FFN_ROW_CHUNKS
    row = lax.broadcasted_iota(jnp.int32, (rc, wg_ref.shape[1]), 0)
    wg, wv, wd = wg_ref[...], wv_ref[...], wd_ref[...]

    def up(r):
        hn = hn_ref[r * rc:(r + 1) * rc, :]
        return _dot(hn, wg), _dot(hn, wv)

    def conv(us, r, idx, cw_ref, cb_ref):
        u = us[r][idx]
        before = us[r - 1][idx][rc - 1:rc, :] if r > 0 else 0.0
        after = us[r + 1][idx][0:1, :] if r + 1 < FFN_ROW_CHUNKS else 0.0
        prev = jnp.where(row == 0, before, pltpu.roll(u, 1, 0))
        nxt = jnp.where(row == rc - 1, after, pltpu.roll(u, rc - 1, 0))
        return prev * cw_ref[0:1, :] + cb_ref[...] + u * cw_ref[1:2, :] + nxt * cw_ref[2:3, :]

    us = [None] * FFN_ROW_CHUNKS
    us[0] = up(0)
    for r in range(FFN_ROW_CHUNKS):
        if r + 1 < FFN_ROW_CHUNKS:
            us[r + 1] = up(r + 1)
        gate = conv(us, r, 0, cwg_ref, cbg_ref)
        val = conv(us, r, 1, cwv_ref, cbv_ref)
        act = (gate * jax.nn.sigmoid(gate) * val).astype(BF16)
        o_ref[0, r * rc:(r + 1) * rc, :] += _dot(act, wd)


def _ffn(x, g, w_up, conv_w, conv_b, w_down, tf=256):
    b, s, d = x.shape
    f = w_down.shape[0]
    nf = f // tf
    wu = w_up.astype(BF16)
    wd = w_down.astype(BF16)
    cb = conv_b.reshape(1, 2 * f)
    return pl.pallas_call(
        _ffn_kernel,
        grid=(b, nf),
        in_specs=[
            pl.BlockSpec((1, s, d), lambda i, j: (i, 0, 0)),
            pl.BlockSpec((1, d), lambda i, j: (0, 0)),
            pl.BlockSpec((d, tf), lambda i, j: (0, j)),
            pl.BlockSpec((d, tf), lambda i, j: (0, nf + j)),
            pl.BlockSpec((3, tf), lambda i, j: (0, j)),
            pl.BlockSpec((3, tf), lambda i, j: (0, nf + j)),
            pl.BlockSpec((1, tf), lambda i, j: (0, j)),
            pl.BlockSpec((1, tf), lambda i, j: (0, nf + j)),
            pl.BlockSpec((tf, d), lambda i, j: (j, 0)),
        ],
        out_specs=pl.BlockSpec((1, s, d), lambda i, j: (i, 0, 0)),
        out_shape=jax.ShapeDtypeStruct((b, s, d), F32),
        scratch_shapes=[pltpu.VMEM((s, d), BF16)],
        compiler_params=_params("parallel", "arbitrary"),
        name="conv_ffn",
    )(x, g.reshape(1, d), wu, wu, conv_w, conv_w, cb, cb, wd)


def _proj_res_kernel(a_ref, w_ref, x_ref, o_ref):
    o_ref[...] = x_ref[...] + _dot(a_ref[...], w_ref[...])


def _proj_res(a, w, x):
    t, k = a.shape
    d = w.shape[1]
    tm = _tiles(t)["proj"]
    return pl.pallas_call(
        _proj_res_kernel,
        grid=(t // tm,),
        in_specs=[
            pl.BlockSpec((tm, k), lambda i: (i, 0)),
            pl.BlockSpec((k, d), lambda i: (0, 0)),
            pl.BlockSpec((tm, d), lambda i: (i, 0)),
        ],
        out_specs=pl.BlockSpec((tm, d), lambda i: (i, 0)),
        out_shape=jax.ShapeDtypeStruct((t, d), F32),
        compiler_params=_params("parallel"),
        name="proj_residual",
    )(a, w.astype(BF16), x)


def _rope_apply(xn, cos_t, sin_up, sin_dn, half):
    return xn * cos_t + pltpu.roll(xn, LANES - half, 1) * sin_up + pltpu.roll(xn, half, 1) * sin_dn


def _rope_tables(s, rot, start, group):
    half = rot // 2
    inv = ROPE_THETA ** (-jnp.arange(0, rot, 2, dtype=F32) / rot)
    ang = jnp.arange(s, dtype=F32)[:, None] * inv[None, :]
    cos, sin = jnp.cos(ang), jnp.sin(ang)
    lane = jnp.arange(LANES) % group
    first = (lane >= start) & (lane < start + half)
    second = (lane >= start + half) & (lane < start + rot)
    idx = jnp.clip(jnp.where(second, lane - start - half, lane - start), 0, half - 1)
    cos_t = jnp.where(first | second, cos[:, idx], 1.0)
    sin_up = jnp.where(first, -sin[:, idx], 0.0)
    sin_dn = jnp.where(second, sin[:, idx], 0.0)
    return cos_t, sin_up, sin_dn


def _swa_qkv_kernel(x_ref, g_ref, w_ref, qg_ref, kg_ref, c_ref, su_ref, sd_ref, q_ref, k_ref, v_ref):
    hn = (_rms(x_ref[...]) * g_ref[...]).astype(BF16)
    y = _dot(hn, w_ref[...])
    tm = y.shape[0]
    nq = q_ref.shape[1]
    nk = k_ref.shape[1]
    m0 = _low_half((tm, LANES))
    cos_t, sin_up, sin_dn = c_ref[...], su_ref[...], sd_ref[...]

    def head_norm_rope(xc, gain):
        ss = _pair_sum(xc * xc, m0)
        xn = xc * lax.rsqrt(ss * (1.0 / HEAD_DIM) + NORM_EPS) * gain
        return _rope_apply(xn, cos_t, sin_up, sin_dn, SWA_ROT // 2)

    for c in range(nq // LANES):
        sl = slice(c * LANES, (c + 1) * LANES)
        q_ref[:, sl] = head_norm_rope(y[:, sl], qg_ref[...]).astype(BF16)
    for c in range(nk // LANES):
        sl = slice(c * LANES, (c + 1) * LANES)
        k_ref[:, sl] = head_norm_rope(y[:, nq + c * LANES:nq + (c + 1) * LANES], kg_ref[...]).astype(BF16)
    v_ref[...] = y[:, nq + nk:].astype(BF16)


def _swa_attn_kernel(sink_ref, q_ref, k_ref, v_ref, o_ref):
    i = pl.program_id(1)
    s = k_ref.shape[1]
    blk = q_ref.shape[1]
    band = 3 * blk
    start = pl.multiple_of(jnp.clip((i - 1) * blk, 0, s - band), blk)
    group = SWA_HEADS // SWA_KV_HEADS
    rows = group * blk
    qpos = i * blk + lax.broadcasted_iota(jnp.int32, (rows, band), 0) % blk
    kpos = start + lax.broadcasted_iota(jnp.int32, (rows, band), 1)
    mask = jnp.abs(qpos - kpos) <= SWA_WINDOW
    m0q = _low_half((blk, LANES))
    rowblk = lax.broadcasted_iota(jnp.int32, (rows, 1), 0) // blk
    zq = jnp.zeros((blk, LANES), BF16)
    kv_groups = range(SWA_KV_HEADS)

    def stacked_q(g):
        parts = []
        for p in range(group // 2):
            qp = q_ref[0, :, (g * (group // 2) + p) * LANES:(g * (group // 2) + p + 1) * LANES]
            parts += [jnp.where(m0q, qp, zq), jnp.where(m0q, zq, qp)]
        return jnp.concatenate(parts, axis=0)

    def sink_col(g):
        sink = jnp.zeros((rows, 1), F32)
        for jh in range(group):
            sink = jnp.where(rowblk == jh, sink_ref[g * group + jh], sink)
        return sink

    band_of = lambda ref, g: ref[0, pl.ds(start, band), g * LANES:(g + 1) * LANES]
    sc = [jnp.where(mask, _dot(stacked_q(g), band_of(k_ref, g), NT), NEG_INF) for g in kv_groups]
    sinks = [sink_col(g) for g in kv_groups]
    mx = [jnp.maximum(jnp.max(s_, axis=-1, keepdims=True), sk) for s_, sk in zip(sc, sinks)]
    pe = [jnp.exp(s_ - m_) for s_, m_ in zip(sc, mx)]
    den = [jnp.sum(p_, axis=-1, keepdims=True) + jnp.exp(sk - m_) for p_, sk, m_ in zip(pe, sinks, mx)]
    ob = [_dot(p_.astype(BF16), band_of(v_ref, g)) / d_ for p_, g, d_ in zip(pe, kv_groups, den)]
    for g in kv_groups:
        for p in range(group // 2):
            o0 = ob[g][(2 * p) * blk:(2 * p + 1) * blk]
            o1 = ob[g][(2 * p + 1) * blk:(2 * p + 2) * blk]
            col = (g * (group // 2) + p) * LANES
            o_ref[0, :, col:col + LANES] = jnp.where(m0q, o0, o1).astype(BF16)


def _swa_layer(x, g, w_qkv, q_gain, k_gain, sinks, w_o):
    b, s, d = x.shape
    t = b * s
    qd = SWA_HEADS * HEAD_DIM
    kd = SWA_KV_HEADS * HEAD_DIM
    wq = w_qkv[:, :qd]
    wk = w_qkv[:, qd:qd + kd].reshape(d, SWA_KV_HEADS, 1, HEAD_DIM)
    wv = w_qkv[:, qd + kd:].reshape(d, SWA_KV_HEADS, 1, HEAD_DIM)
    dup = lambda w: jnp.broadcast_to(w, (d, SWA_KV_HEADS, 2, HEAD_DIM)).reshape(d, 2 * kd)
    w_all = jnp.concatenate([wq, dup(wk), dup(wv)], axis=1).astype(BF16)
    nk = 2 * kd
    qg = (jnp.tile(q_gain, 2) * HEAD_DIM ** -0.5).reshape(1, LANES)
    kg = jnp.tile(k_gain, 2).reshape(1, LANES)
    cos_t, sin_up, sin_dn = _rope_tables(s, SWA_ROT, 0, HEAD_DIM)
    tm = _tiles(t)["proj"]
    tm = min(tm, s)
    nt = s // tm
    tab = pl.BlockSpec((tm, LANES), lambda i: (i % nt, 0))
    q, k, v = pl.pallas_call(
        _swa_qkv_kernel,
        grid=(t // tm,),
        in_specs=[
            pl.BlockSpec((tm, d), lambda i: (i, 0)),
            pl.BlockSpec((1, d), lambda i: (0, 0)),
            pl.BlockSpec((d, qd + 2 * nk), lambda i: (0, 0)),
            pl.BlockSpec((1, LANES), lambda i: (0, 0)),
            pl.BlockSpec((1, LANES), lambda i: (0, 0)),
            tab, tab, tab,
        ],
        out_specs=[
            pl.BlockSpec((tm, qd), lambda i: (i, 0)),
            pl.BlockSpec((tm, nk), lambda i: (i, 0)),
            pl.BlockSpec((tm, nk), lambda i: (i, 0)),
        ],
        out_shape=[jax.ShapeDtypeStruct((t, qd), BF16), jax.ShapeDtypeStruct((t, nk), BF16),
                   jax.ShapeDtypeStruct((t, nk), BF16)],
        compiler_params=_params("parallel"),
        name="swa_qkv",
    )(x.reshape(t, d), g.reshape(1, d), w_all, qg, kg, cos_t, sin_up, sin_dn)

    nb = s // SWA_BLOCK
    o = pl.pallas_call(
        _swa_attn_kernel,
        grid_spec=pltpu.PrefetchScalarGridSpec(
            num_scalar_prefetch=1,
            grid=(b, nb),
            in_specs=[
                pl.BlockSpec((1, SWA_BLOCK, qd), lambda bi, i, sk: (bi, i, 0)),
                pl.BlockSpec((1, s, nk), lambda bi, i, sk: (bi, 0, 0)),
                pl.BlockSpec((1, s, nk), lambda bi, i, sk: (bi, 0, 0)),
            ],
            out_specs=pl.BlockSpec((1, SWA_BLOCK, qd), lambda bi, i, sk: (bi, i, 0)),
        ),
        out_shape=jax.ShapeDtypeStruct((b, s, qd), BF16),
        compiler_params=_params("parallel", "arbitrary"),
        name="swa_attention",
    )(sinks.astype(F32), q.reshape(b, s, qd), k.reshape(b, s, nk), v.reshape(b, s, nk))
    return _proj_res(o.reshape(t, qd), w_o, x.reshape(t, d)).reshape(b, s, d)


def _mla_proj_kernel(x_ref, g_ref, wd_ref, cqg_ref, ckvg_ref, wuq_ref, wuk_ref, wuv_ref, qg_ref, kg_ref,
                     c_ref, su_ref, sd_ref, q_ref, k_ref, v_ref):
    hn = (_rms(x_ref[...]) * g_ref[...]).astype(BF16)
    down = _dot(hn, wd_ref[...])
    cq = (_rms(down[:, :MLA_Q_LORA]) * cqg_ref[...]).astype(BF16)
    ckv = (_rms(down[:, MLA_Q_LORA:MLA_Q_LORA + MLA_KV_LORA]) * ckvg_ref[...]).astype(BF16)
    tail = down[:, MLA_Q_LORA + MLA_KV_LORA:]
    k_rope = pltpu.roll(tail, MLA_NOPE, 1)
    qp = _dot(cq, wuq_ref[...])
    kp = _dot(ckv, wuk_ref[...])
    v_ref[...] = _dot(ckv, wuv_ref[...]).astype(BF16)
    cos_t, sin_up, sin_dn = c_ref[...], su_ref[...], sd_ref[...]
    inv_dim = 1.0 / (MLA_NOPE + MLA_ROPE)

    def head_norm_rope(xh, gain):
        ss = jnp.sum(xh * xh, axis=-1, keepdims=True)
        xn = xh * lax.rsqrt(ss * inv_dim + NORM_EPS) * gain
        return _rope_apply(xn, cos_t, sin_up, sin_dn, MLA_ROPE // 2)

    for h in range(MLA_HEADS):
        sl = slice(h * LANES, (h + 1) * LANES)
        q_ref[:, sl] = head_norm_rope(qp[:, sl], qg_ref[...]).astype(BF16)
        k_ref[:, sl] = head_norm_rope(kp[:, sl] + k_rope, kg_ref[...]).astype(BF16)


def _mla_attn_kernel(q_ref, k_ref, v_ref, o_ref, sc_ref, mx_ref):
    i = pl.program_id(2)
    s = k_ref.shape[1]
    kc = s // MLA_KEY_CHUNKS
    heads = range(2)
    chunks = range(MLA_KEY_CHUNKS)
    lanes = [slice(m * LANES, (m + 1) * LANES) for m in heads]

    @pl.when(i == 0)
    def _():
        sc_ref[...] = jnp.zeros(sc_ref.shape, F32)
        mx_ref[...] = jnp.zeros(mx_ref.shape, F32)

    rows = lambda c: slice(c * kc, (c + 1) * kc)
    den = [0.0, 0.0]
    ot = [0.0, 0.0]
    mx_old = [mx_ref[m] for m in heads]
    mx_new = [None, None]
    for c in chunks:
        for m in heads:
            pe = jnp.exp(sc_ref[m, rows(c), :] - mx_old[m])
            den[m] = den[m] + jnp.sum(pe, axis=0, keepdims=True)
            ot[m] = ot[m] + _dot(v_ref[0, rows(c), :], pe.astype(BF16), TN)
        for m in heads:
            sc = _dot(k_ref[0, rows(c), lanes[m]], q_ref[0, :, lanes[m]], NT)
            sc_ref[m, rows(c), :] = sc
            cmax = jnp.max(sc, axis=0, keepdims=True)
            mx_new[m] = cmax if c == 0 else jnp.maximum(mx_new[m], cmax)
    for m in heads:
        mx_ref[m] = mx_new[m]
    halves = [(ot[m] / den[m])[m * MLA_V:(m + 1) * MLA_V] for m in heads]
    o_ref[0] = jnp.concatenate(halves, axis=0).T.astype(BF16)


def _mla_layer(x, g, w_down, cq_gain, ckv_gain, w_uq, w_ukv, q_gain, k_gain, w_o):
    b, s, d = x.shape
    t = b * s
    hd = MLA_NOPE + MLA_ROPE
    n_down = w_down.shape[1]
    wd = jnp.pad(w_down, ((0, 0), (0, MLA_DOWN_PAD - n_down))).astype(BF16)
    pad_heads = lambda w, width: jnp.pad(w, ((0, 0), (0, 0), (0, LANES - width))).reshape(w.shape[0], MLA_HEADS * LANES)
    wuq = pad_heads(w_uq.reshape(MLA_Q_LORA, MLA_HEADS, hd), hd).astype(BF16)
    wkv = w_ukv.reshape(MLA_KV_LORA, MLA_HEADS, MLA_NOPE + MLA_V)
    wuk = pad_heads(wkv[:, :, :MLA_NOPE], MLA_NOPE).astype(BF16)
    wuv = wkv[:, :, MLA_NOPE:].reshape(MLA_KV_LORA, MLA_HEADS * MLA_V).astype(BF16)
    qg = jnp.pad(q_gain * hd ** -0.5, (0, LANES - hd)).reshape(1, LANES)
    kg = jnp.pad(k_gain, (0, LANES - hd)).reshape(1, LANES)
    cos_t, sin_up, sin_dn = _rope_tables(s, MLA_ROPE, MLA_NOPE, LANES)
    tiles = _tiles(t)
    tm = min(tiles["mla_proj"], s)
    nt = s // tm
    nq = MLA_HEADS * LANES
    nv = MLA_HEADS * MLA_V
    full = lambda shape: pl.BlockSpec(shape, lambda i: (0, 0))
    tab = pl.BlockSpec((tm, LANES), lambda i: (i % nt, 0))
    q, k, v = pl.pallas_call(
        _mla_proj_kernel,
        grid=(t // tm,),
        in_specs=[
            pl.BlockSpec((tm, d), lambda i: (i, 0)),
            full((1, d)), full((d, MLA_DOWN_PAD)), full((1, MLA_Q_LORA)), full((1, MLA_KV_LORA)),
            full((MLA_Q_LORA, nq)), full((MLA_KV_LORA, nq)), full((MLA_KV_LORA, nv)),
            full((1, LANES)), full((1, LANES)), tab, tab, tab,
        ],
        out_specs=[
            pl.BlockSpec((tm, nq), lambda i: (i, 0)),
            pl.BlockSpec((tm, nq), lambda i: (i, 0)),
            pl.BlockSpec((tm, nv), lambda i: (i, 0)),
        ],
        out_shape=[jax.ShapeDtypeStruct((t, nq), BF16), jax.ShapeDtypeStruct((t, nq), BF16),
                   jax.ShapeDtypeStruct((t, nv), BF16)],
        compiler_params=_params("parallel"),
        name="mla_proj",
    )(x.reshape(t, d), g.reshape(1, d), wd, cq_gain.reshape(1, -1), ckv_gain.reshape(1, -1), wuq, wuk, wuv,
      qg, kg, cos_t, sin_up, sin_dn)

    tq = min(tiles["mla_q"], s)
    n_q = s // tq
    o = pl.pallas_call(
        _mla_attn_kernel,
        grid=(b, MLA_HEADS // 2, n_q + 1),
        in_specs=[
            pl.BlockSpec((1, tq, 2 * LANES), lambda bi, p, i: (bi, jnp.minimum(i, n_q - 1), p)),
            pl.BlockSpec((1, s, 2 * LANES), lambda bi, p, i: (bi, 0, p)),
            pl.BlockSpec((1, s, LANES), lambda bi, p, i: (bi, 0, p)),
        ],
        out_specs=pl.BlockSpec((1, tq, LANES), lambda bi, p, i: (bi, jnp.maximum(i - 1, 0), p)),
        out_shape=jax.ShapeDtypeStruct((b, s, nv), BF16),
        scratch_shapes=[pltpu.VMEM((2, s, tq), F32), pltpu.VMEM((2, 1, tq), F32)],
        compiler_params=_params("parallel", "parallel", "arbitrary"),
        name="mla_attention",
    )(q.reshape(b, s, nq), k.reshape(b, s, nq), v.reshape(b, s, nv))
    return _proj_res(o.reshape(t, nv), w_o, x.reshape(t, d)).reshape(b, s, d)


def _rwkv_proj_kernel(x_ref, xp_ref, xn_ref, g_ref, mu_ref, wr_ref, wk_ref, wv_ref, g1_ref, g2_ref,
                      w1_ref, w2_ref, w0_ref, a1_ref, a2_ref, a0_ref,
                      r_ref, k_ref, v_ref, gt_ref, lwf_ref, lwb_ref, af_ref, ab_ref, *, tiles_per_seq):
    i = pl.program_id(0)
    gain = g_ref[...]
    h = _rms(x_ref[...]) * gain
    tm = h.shape[0]
    first = (i % tiles_per_seq) == 0
    last = (i % tiles_per_seq) == tiles_per_seq - 1
    h_before = jnp.where(first, 0.0, _rms(xp_ref[7:8, :]) * gain)
    h_after = jnp.where(last, 0.0, _rms(xn_ref[0:1, :]) * gain)
    row = lax.broadcasted_iota(jnp.int32, h.shape, 0)
    h_prev = jnp.where(row == 0, h_before, pltpu.roll(h, 1, 0))
    h_next = jnp.where(row == tm - 1, h_after, pltpu.roll(h, tm - 1, 0))
    xx = 0.5 * (h_prev + h_next) - h
    mix = lambda n: (h + xx * mu_ref[n:n + 1, :]).astype(BF16)
    r_ref[...] = _dot(mix(0), wr_ref[...]).astype(r_ref.dtype)
    k_ref[...] = _dot(mix(2), wk_ref[...]).astype(k_ref.dtype)
    v_ref[...] = _dot(mix(3), wv_ref[...]).astype(v_ref.dtype)
    gt_ref[...] = _dot(jax.nn.sigmoid(_dot(mix(5), g1_ref[...])).astype(BF16), g2_ref[...]).astype(gt_ref.dtype)
    d = h.shape[1]
    wl = w0_ref[...] + _dot(jnp.tanh(_dot(mix(1), w1_ref[...])).astype(BF16), w2_ref[...])
    lw = -jnp.exp(-0.5) * jax.nn.sigmoid(wl)
    lwf_ref[...] = lw[:, :d]
    lwb_ref[...] = lw[:, d:]
    al = jax.nn.sigmoid(a0_ref[...] + _dot(_dot(mix(4), a1_ref[...]).astype(BF16), a2_ref[...]))
    af_ref[...] = al[:, :d].astype(af_ref.dtype)
    ab_ref[...] = al[:, d:].astype(ab_ref.dtype)


def _wkv_masks(c, rev):
    n = 2 * WKV_PAIRS * c
    r = lax.broadcasted_iota(jnp.int32, (n, WKV_PAIRS * c), 0)
    q = lax.broadcasted_iota(jnp.int32, (n, WKV_PAIRS * c), 1)
    i, j = r % c, q % c
    same_pair = ((r // c) % WKV_PAIRS) == (q // c)
    upto = same_pair & ((j >= i) if rev else (j <= i))
    before = same_pair & ((j > i) if rev else (j < i))
    half = r // (WKV_PAIRS * c)
    rn = lax.broadcasted_iota(jnp.int32, (n, n), 0)
    cn = lax.broadcasted_iota(jnp.int32, (n, n), 1)
    ns = WKV_PAIRS * LANES
    r2 = lax.broadcasted_iota(jnp.int32, (ns, ns), 0)
    c2 = lax.broadcasted_iota(jnp.int32, (ns, ns), 1)
    ri = lax.broadcasted_iota(jnp.int32, (c, c), 0)
    ci = lax.broadcasted_iota(jnp.int32, (c, c), 1)
    sizes = []
    s = 2
    while s < c:
        sizes.append(s)
        s *= 2
    return {
        "upto": [upto & (half == m) for m in range(2)],
        "before": [before & (half == m) for m in range(2)],
        "join": [((rn // (2 * s)) == (cn // (2 * s))) & ((rn // s) != (cn // s)) for s in sizes],
        "base": (rn // 2) == (cn // 2),
        "eye": jnp.where(rn == cn, 1.0, 0.0),
        "diag": r2 == c2,
        "same_head": (r2 // HEAD_DIM) == (c2 // HEAD_DIM),
        "cum": jnp.where((ci >= ri) if rev else (ci <= ri), 1.0, 0.0),
    }


def _cat_l(parts):
    return jnp.concatenate(parts, axis=1)


def _cat_r(parts):
    return jnp.concatenate(parts, axis=0)


def _pair(t, p):
    return t[:, p * LANES:(p + 1) * LANES]


def _stack_pairs(t, reps):
    return _cat_r([_pair(t, p) for _ in range(reps) for p in range(WKV_PAIRS)])


def _wkv_prepare(r, lw, k, v, asig, k_k, k_a, rev, masks):
    c = r.shape[0]
    m0p = _low_half((c, LANES))
    pairs = range(WKV_PAIRS)
    pair, cat_l, cat_r = _pair, _cat_l, _cat_r

    kk = k * k_k
    kk2 = kk * kk
    kk = kk / jnp.maximum(jnp.sqrt(cat_l([_pair_sum(pair(kk2, p), m0p) for p in pairs])), 1e-12)
    bvec = kk * asig
    kd = k * (1.0 + (asig - 1.0) * k_a)

    cl = _mm_exact_lhs(masks["cum"], lw)
    tot = cl[0:1, :] if rev else cl[c - 1:c, :]
    g_inv = jnp.exp(-cl)
    g_end = jnp.exp(tot - cl)
    at = -kk * jnp.exp(cl - lw)
    rt = r * jnp.exp(cl)
    bt = bvec * g_inv
    kt = kd * g_inv
    b_end = bvec * g_end
    k_end = kd * g_end

    lane = lax.broadcasted_iota(jnp.int32, r.shape, 1)
    low = lane % LANES < HEAD_DIM
    lhs = cat_r([jnp.where(low, at, 0.0), jnp.where(low, 0.0, at), jnp.where(low, rt, 0.0), jnp.where(low, 0.0, rt)])
    rhs = cat_r([jnp.where(lane // LANES == p, t, 0.0) for t in (bt, kt) for p in pairs])
    sc = _mm(lhs, rhs, NT, WKV_SCORE_PASSES)
    pc = WKV_PAIRS * c

    def block_diag(lhs_kind, rhs_kind, key):
        s = sc[lhs_kind * 2 * c:(lhs_kind + 1) * 2 * c, rhs_kind * pc:(rhs_kind + 1) * pc]
        st = cat_r([s[m * c:(m + 1) * c] for m in range(2) for _ in pairs])
        return cat_l([jnp.where(masks[key][m], st, 0.0) for m in range(2)])

    return {"a_ab": block_diag(0, 0, "before"), "a_rb": block_diag(1, 0, "upto"),
            "a_ak": block_diag(0, 1, "before"), "a_rk": block_diag(1, 1, "upto"),
            "at": at, "rt": rt, "v": v, "b_end": b_end, "k_end": k_end, "decay": jnp.exp(tot)}


def _wkv_chunks(units):
    pres = [u[0] for u in units]
    states = [u[1] for u in units]
    masks = [u[2] for u in units]
    c = pres[0]["at"].shape[0]
    m0p = _low_half((c, LANES))
    inv = [mk["eye"] + jnp.where(mk["base"], p["a_ab"], 0.0) for p, mk in zip(pres, masks)]
    for lvl in range(len(masks[0]["join"])):
        ex = [_mm(jnp.where(mk["join"][lvl], p["a_ab"], 0.0), x, NN, WKV_INV_PASSES)
              for p, mk, x in zip(pres, masks, inv)]
        inv = [x + _mm(x, t, NN, WKV_INV_PASSES) for x, t in zip(inv, ex)]
    v_st = [_stack_pairs(p["v"], 2) for p in pres]
    x_ak = [_mm(p["a_ak"], vs) for p, vs in zip(pres, v_st)]
    yk = [_mm(p["a_rk"], vs) for p, vs in zip(pres, v_st)]
    zz = [_mm(x, _cat_l([_stack_pairs(p["at"], 2), xa]), NN, WKV_INV_PASSES)
          for x, p, xa in zip(inv, pres, x_ak)]

    def unstack(t):
        return _cat_l([jnp.where(m0p, t[p * c:(p + 1) * c], t[(WKV_PAIRS + p) * c:(WKV_PAIRS + p + 1) * c])
                       for p in range(WKV_PAIRS)])

    w_a = [unstack(z[:, :LANES]) for z in zz]
    u_0 = [unstack(z[:, LANES:]) for z in zz]
    hw = [_mm(_cat_r([w, p["rt"]]), st, NN, WKV_STATE_PASSES) for w, p, st in zip(w_a, pres, states)]
    u = [h[:c] + u0 for h, u0 in zip(hw, u_0)]
    yu = [_mm(p["a_rb"], _stack_pairs(uu, 2)) for p, uu in zip(pres, u)]
    grow = [_mm(_cat_r([p["b_end"], p["k_end"]]), _cat_r([uu, p["v"]]), TN) for p, uu in zip(pres, u)]
    out = []
    for p, mk, st, h, yuu, ykk, gr in zip(pres, masks, states, hw, yu, yk, grow):
        y = h[c:] + unstack(yuu + ykk)
        decay_col = jnp.sum(jnp.where(mk["diag"], p["decay"], 0.0), axis=-1, keepdims=True)
        out.append((y, st * decay_col + jnp.where(mk["same_head"], gr, 0.0)))
    return out


def _rwkv_scan_kernel(rf_ref, kf_ref, vf_ref, rb_ref, kb_ref, vb_ref, lwf_ref, af_ref, lwb_ref, ab_ref,
                      kk_ref, ka_ref, yf_ref, yb_ref, st_ref, *, groups):
    @pl.when(pl.program_id(2) == 0)
    def _():
        st_ref[...] = jnp.zeros_like(st_ref)

    c = rf_ref.shape[1]
    width = WKV_PAIRS * LANES
    dirs = ((rf_ref, kf_ref, vf_ref, lwf_ref, af_ref, False), (rb_ref, kb_ref, vb_ref, lwb_ref, ab_ref, True))
    units = []
    for di, (r_ref, k_ref, v_ref, lw_ref, a_ref, rev) in enumerate(dirs):
        masks = _wkv_masks(c, rev)
        for gi in range(groups):
            sl = slice(gi * width, (gi + 1) * width)
            f32 = lambda ref: ref[0, :, sl].astype(F32)
            pre = _wkv_prepare(f32(r_ref), lw_ref[0, :, sl], f32(k_ref), f32(v_ref), f32(a_ref),
                               kk_ref[:, sl], ka_ref[:, sl], rev, masks)
            units.append((pre, st_ref[di, gi], masks))
    results = _wkv_chunks(units)
    for ui, (y, st) in enumerate(results):
        di, gi = divmod(ui, groups)
        (yf_ref, yb_ref)[di][0, :, gi * width:(gi + 1) * width] = y.astype(yf_ref.dtype)
        st_ref[di, gi] = st


def _rwkv_out_kernel(x_ref, r_ref, k_ref, v_ref, af_ref, ab_ref, gt_ref, yf_ref, yb_ref,
                     ka_ref, rk_ref, lw_ref, lb_ref, wo_ref, o_ref, act_ref):
    tm = x_ref.shape[0]
    m0 = _low_half((tm, LANES))
    inv_n = 1.0 / HEAD_DIM
    for c in range(x_ref.shape[1] // LANES):
        sl = slice(c * LANES, (c + 1) * LANES)
        f32 = lambda ref: ref[:, sl].astype(F32)
        r, k, v = f32(r_ref), f32(k_ref), f32(v_ref)
        k_a = ka_ref[:, sl]
        kd_f = k * (1.0 + (f32(af_ref) - 1.0) * k_a)
        kd_b = k * (1.0 + (f32(ab_ref) - 1.0) * k_a)
        rr = r * rk_ref[:, sl]
        bonus = (_pair_sum(rr * kd_f, m0) + _pair_sum(rr * kd_b, m0)) * v
        y = f32(yf_ref) + f32(yb_ref)
        dev = y - _pair_sum(y, m0) * inv_n
        var = _pair_sum(dev * dev, m0) * inv_n
        yn = dev * lax.rsqrt(var + RWKV_GN_EPS) * lw_ref[:, sl] + lb_ref[:, sl]
        act_ref[:, sl] = ((yn + bonus) * gt_ref[:, sl]).astype(BF16)
    o_ref[...] = x_ref[...] + _dot(act_ref[...], wo_ref[...])


def _rwkv_layer(x, g, mu, w_r, w_k, w_v, w0, w1, w2, a0, a1, a2, g1, g2, k_k, k_a, r_k, lnx_w, lnx_b, w_o):
    b, s, d = x.shape
    t = b * s
    tiles = _tiles(t)
    tm = min(tiles["rwkv_proj"], s)
    xf = x.reshape(t, d)
    lora_g = g1.shape[1]
    gpad = -lora_g % LANES
    g1p = jnp.pad(g1, ((0, 0), (0, gpad))).astype(BF16)
    g2p = jnp.pad(g2, ((0, gpad), (0, 0))).astype(BF16)

    def both_dirs(m1, m2):
        l = m1.shape[2]
        z = jnp.zeros((l, d), m2.dtype)
        cat = jnp.concatenate([m1[0], m1[1]], axis=1)
        bd = jnp.concatenate([jnp.concatenate([m2[0], z], axis=1), jnp.concatenate([z, m2[1]], axis=1)], axis=0)
        return cat.astype(BF16), bd.astype(BF16)

    w1c, w2c = both_dirs(w1, w2)
    a1c, a2c = both_dirs(a1, a2)
    full = lambda arr: pl.BlockSpec(arr.shape, lambda i: (0,) * arr.ndim)
    halo = tm // 8
    n8 = t // 8
    tok = pl.BlockSpec((tm, d), lambda i: (i, 0))
    weights = [g.reshape(1, d), mu, w_r.astype(BF16), w_k.astype(BF16), w_v.astype(BF16), g1p, g2p,
               w1c, w2c, w0.reshape(1, 2 * d), a1c, a2c, a0.reshape(1, 2 * d)]
    outs = pl.pallas_call(
        functools.partial(_rwkv_proj_kernel, tiles_per_seq=s // tm),
        grid=(t // tm,),
        in_specs=[tok,
                  pl.BlockSpec((8, d), lambda i: (jnp.maximum(i * halo - 1, 0), 0)),
                  pl.BlockSpec((8, d), lambda i: (jnp.minimum((i + 1) * halo, n8 - 1), 0)),
                  ] + [full(w) for w in weights],
        out_specs=[tok] * 8,
        out_shape=[jax.ShapeDtypeStruct((t, d), dt) for dt in (BF16, BF16, BF16, BF16, F32, F32, BF16, BF16)],
        compiler_params=_params("parallel"),
        name="rwkv_proj",
    )(xf, xf, xf, *weights)
    r, k, v, gate, lw_f, lw_b, a_f, a_b = [o.reshape(b, s, d) for o in outs]

    c = RWKV_CHUNK
    nc = s // c
    groups = WKV_GROUPS_PER_STEP
    wpair = groups * WKV_PAIRS * LANES
    n_state = 2 * WKV_PAIRS * HEAD_DIM
    fwd = pl.BlockSpec((1, c, wpair), lambda bi, p, ci: (bi, ci, p))
    bwd = pl.BlockSpec((1, c, wpair), lambda bi, p, ci: (bi, nc - 1 - ci, p))
    vec = pl.BlockSpec((1, wpair), lambda bi, p, ci: (0, p))
    y_f, y_b = pl.pallas_call(
        functools.partial(_rwkv_scan_kernel, groups=groups),
        grid=(b, d // wpair, nc),
        in_specs=[fwd, fwd, fwd, bwd, bwd, bwd, fwd, fwd, bwd, bwd, vec, vec],
        out_specs=[fwd, bwd],
        out_shape=[jax.ShapeDtypeStruct((b, s, d), BF16)] * 2,
        scratch_shapes=[pltpu.VMEM((2, groups, n_state, n_state), F32)],
        compiler_params=_params("parallel", "parallel", "arbitrary"),
        name="rwkv_scan",
    )(r, k, v, r, k, v, lw_f, a_f, lw_b, a_b, k_k.reshape(1, d), k_a.reshape(1, d))

    flat = lambda z: z.reshape(t, d)
    rowv = lambda z: z.reshape(1, d)
    tmo = tiles["rwkv_proj"]
    tok_o = pl.BlockSpec((tmo, d), lambda i: (i, 0))
    vec_o = pl.BlockSpec((1, d), lambda i: (0, 0))
    out = pl.pallas_call(
        _rwkv_out_kernel,
        grid=(t // tmo,),
        in_specs=[tok_o] * 9 + [vec_o] * 4 + [pl.BlockSpec((d, d), lambda i: (0, 0))],
        out_specs=tok_o,
        out_shape=jax.ShapeDtypeStruct((t, d), F32),
        scratch_shapes=[pltpu.VMEM((tmo, d), BF16)],
        compiler_params=_params("parallel"),
        name="rwkv_out",
    )(xf, flat(r), flat(k), flat(v), flat(a_f), flat(a_b), flat(gate), flat(y_f), flat(y_b),
      rowv(k_a), rowv(r_k), rowv(lnx_w), rowv(lnx_b), w_o.astype(BF16))
    return out.reshape(b, s, d)


def kernel(x, norm_tok, norm_ch, ffn_w_up, ffn_conv_w, ffn_conv_b, ffn_w_down, swa_w_qkv, swa_q_gain, swa_k_gain, swa_sinks, swa_w_o, rwkv_mu, rwkv_w_r, rwkv_w_k, rwkv_w_v, rwkv_w0, rwkv_w1, rwkv_w2, rwkv_a0, rwkv_a1, rwkv_a2, rwkv_g1, rwkv_g2, rwkv_k_k, rwkv_k_a, rwkv_r_k, rwkv_lnx_w, rwkv_lnx_b, rwkv_w_o, mla_w_down, mla_cq_gain, mla_ckv_gain, mla_w_uq, mla_w_ukv, mla_q_gain, mla_k_gain, mla_w_o):
    depth = norm_tok.shape[0]
    for i in range(depth):
        kind = i % N_MIXERS
        j = i // N_MIXERS
        if kind == 0:
            x = _swa_layer(x, norm_tok[i], swa_w_qkv[j], swa_q_gain[j], swa_k_gain[j], swa_sinks[j], swa_w_o[j])
        elif kind == 1:
            x = _rwkv_layer(x, norm_tok[i], rwkv_mu[j], rwkv_w_r[j], rwkv_w_k[j], rwkv_w_v[j], rwkv_w0[j],
                            rwkv_w1[j], rwkv_w2[j], rwkv_a0[j], rwkv_a1[j], rwkv_a2[j], rwkv_g1[j], rwkv_g2[j],
                            rwkv_k_k[j], rwkv_k_a[j], rwkv_r_k[j], rwkv_lnx_w[j], rwkv_lnx_b[j], rwkv_w_o[j])
        else:
            x = _mla_layer(x, norm_tok[i], mla_w_down[j], mla_cq_gain[j], mla_ckv_gain[j], mla_w_uq[j],
                           mla_w_ukv[j], mla_q_gain[j], mla_k_gain[j], mla_w_o[j])
        x = _ffn(x, norm_ch[i], ffn_w_up[i], ffn_conv_w[i], ffn_conv_b[i], ffn_w_down[i])
    return x
```

```python
import functools

import jax
import jax.numpy as jnp
from jax import lax
from jax.experimental import pallas as pl
from jax.experimental.pallas import tpu as pltpu

F32 = jnp.float32
BF16 = jnp.bfloat16

N_MIXERS = 3
ROPE_THETA = 500000.0
NORM_EPS = 1e-6
NEG_INF = -1e30
LANES = 128
HEAD_DIM = 64
SWA_HEADS = 16
SWA_KV_HEADS = 4
SWA_WINDOW = 128
SWA_BLOCK = 128
SWA_ROT = HEAD_DIM // 4
FFN_ROW_CHUNKS = 1
RWKV_GN_EPS = 64e-5
RWKV_CHUNK = 64
WKV_PAIRS = 2
WKV_GROUPS_PER_STEP = 4
WKV_SCORE_PASSES = 1
WKV_INV_PASSES = 1
WKV_STATE_PASSES = 1
MLA_HEADS = 16
MLA_NOPE = 64
MLA_ROPE = 32
MLA_V = 64
MLA_Q_LORA = 384
MLA_KV_LORA = 256
MLA_DOWN_PAD = 768
MLA_KEY_CHUNKS = 4
MLA_PAIRS_PER_STEP = 2
VMEM_LIMIT_BYTES = 56 * 1024 * 1024

NN = ((1,), (0,))
NT = ((1,), (1,))
TN = ((0,), (0,))


def _dot(a, b, dims=NN):
    return lax.dot_general(a, b, (dims, ((), ())), preferred_element_type=F32)


def _split3(x):
    hi = x.astype(BF16)
    r1 = x - hi.astype(F32)
    mid = r1.astype(BF16)
    lo = (r1 - mid.astype(F32)).astype(BF16)
    return hi, mid, lo


def _mm(a, b, dims=NN, passes=1):
    if passes == 1:
        return _dot(a.astype(BF16), b.astype(BF16), dims)
    ah, am, _ = _split3(a)
    bh, bm, _ = _split3(b)
    return _dot(ah, bh, dims) + (_dot(ah, bm, dims) + _dot(am, bh, dims))


def _mm_exact_lhs(a01, b):
    bh, bm, bl = _split3(b)
    a = a01.astype(BF16)
    return _dot(a, bh) + (_dot(a, bm) + _dot(a, bl))


def _rms(x, eps=NORM_EPS):
    return x * lax.rsqrt(jnp.mean(x * x, axis=-1, keepdims=True) + eps)


def _low_half(shape):
    return lax.broadcasted_iota(jnp.int32, shape, 1) < HEAD_DIM


def _pair_sum(x, m0):
    s0 = jnp.sum(jnp.where(m0, x, 0.0), axis=-1, keepdims=True)
    s1 = jnp.sum(jnp.where(m0, 0.0, x), axis=-1, keepdims=True)
    return jnp.where(m0, s0, s1)


def _params(*sem):
    return pltpu.CompilerParams(dimension_semantics=sem, vmem_limit_bytes=VMEM_LIMIT_BYTES)


def _tiles(rows):
    def pick(pref):
        t = pref
        while rows % t:
            t //= 2
        return t
    return {"proj": pick(512), "rwkv_proj": pick(256), "mla_proj": pick(256), "mla_q": pick(256)}


def _ffn_kernel(x_ref, g_ref, wg_ref, wv_ref, cwg_ref, cwv_ref, cbg_ref, cbv_ref, wd_ref, o_ref, hn_ref):
    j = pl.program_id(1)

    @pl.when(j == 0)
    def _():
        x = x_ref[0]
        hn_ref[...] = (_rms(x) * g_ref[...]).astype(BF16)
        o_ref[0] = x

    s = hn_ref.shape[0]
    rc = s // FFN_ROW_CHUNKS
    row = lax.broadcasted_iota(jnp.int32, (rc, wg_ref.shape[1]), 0)
    wg, wv, wd = wg_ref[...].astype(BF16), wv_ref[...].astype(BF16), wd_ref[...].astype(BF16)

    def up(r):
        hn = hn_ref[r * rc:(r + 1) * rc, :]
        return _dot(hn, wg), _dot(hn, wv)

    def conv(us, r, idx, cw_ref, cb_ref):
        u = us[r][idx]
        before = us[r - 1][idx][rc - 1:rc, :] if r > 0 else 0.0
        after = us[r + 1][idx][0:1, :] if r + 1 < FFN_ROW_CHUNKS else 0.0
        prev = jnp.where(row == 0, before, pltpu.roll(u, 1, 0))
        nxt = jnp.where(row == rc - 1, after, pltpu.roll(u, rc - 1, 0))
        return prev * cw_ref[0:1, :] + cb_ref[...] + u * cw_ref[1:2, :] + nxt * cw_ref[2:3, :]

    us = [None] * FFN_ROW_CHUNKS
    us[0] = up(0)
    for r in range(FFN_ROW_CHUNKS):
        if r + 1 < FFN_ROW_CHUNKS:
            us[r + 1] = up(r + 1)
        gate = conv(us, r, 0, cwg_ref, cbg_ref)
        val = conv(us, r, 1, cwv_ref, cbv_ref)
        act = (gate * jax.nn.sigmoid(gate) * val).astype(BF16)
        o_ref[0, r * rc:(r + 1) * rc, :] += _dot(act, wd)


def _ffn(x, g, w_up, conv_w, conv_b, w_down, tf=256):
    b, s, d = x.shape
    f = w_down.shape[0]
    nf = f // tf
    wu, wd = w_up, w_down
    cb = conv_b.reshape(1, 2 * f)
    return pl.pallas_call(
        _ffn_kernel,
        grid=(b, nf),
        in_specs=[
            pl.BlockSpec((1, s, d), lambda i, j: (i, 0, 0)),
            pl.BlockSpec((1, d), lambda i, j: (0, 0)),
            pl.BlockSpec((d, tf), lambda i, j: (0, j)),
            pl.BlockSpec((d, tf), lambda i, j: (0, nf + j)),
            pl.BlockSpec((3, tf), lambda i, j: (0, j)),
            pl.BlockSpec((3, tf), lambda i, j: (0, nf + j)),
            pl.BlockSpec((1, tf), lambda i, j: (0, j)),
            pl.BlockSpec((1, tf), lambda i, j: (0, nf + j)),
            pl.BlockSpec((tf, d), lambda i, j: (j, 0)),
        ],
        out_specs=pl.BlockSpec((1, s, d), lambda i, j: (i, 0, 0)),
        out_shape=jax.ShapeDtypeStruct((b, s, d), F32),
        scratch_shapes=[pltpu.VMEM((s, d), BF16)],
        compiler_params=_params("parallel", "arbitrary"),
        name="conv_ffn",
    )(x, g.reshape(1, d), wu, wu, conv_w, conv_w, cb, cb, wd)


def _proj_res_kernel(a_ref, w_ref, x_ref, o_ref):
    o_ref[...] = x_ref[...] + _dot(a_ref[...], w_ref[...])


def _proj_res(a, w, x):
    t, k = a.shape
    d = w.shape[1]
    tm = _tiles(t)["proj"]
    return pl.pallas_call(
        _proj_res_kernel,
        grid=(t // tm,),
        in_specs=[
            pl.BlockSpec((tm, k), lambda i: (i, 0)),
            pl.BlockSpec((k, d), lambda i: (0, 0)),
            pl.BlockSpec((tm, d), lambda i: (i, 0)),
        ],
        out_specs=pl.BlockSpec((tm, d), lambda i: (i, 0)),
        out_shape=jax.ShapeDtypeStruct((t, d), F32),
        compiler_params=_params("parallel"),
        name="proj_residual",
    )(a, w.astype(BF16), x)


def _rope_apply(xn, cos_t, sin_up, sin_dn, half):
    return xn * cos_t + pltpu.roll(xn, LANES - half, 1) * sin_up + pltpu.roll(xn, half, 1) * sin_dn


def _rope_tables(s, rot, start, group):
    half = rot // 2
    inv = ROPE_THETA ** (-jnp.arange(0, rot, 2, dtype=F32) / rot)
    ang = jnp.arange(s, dtype=F32)[:, None] * inv[None, :]
    cos, sin = jnp.cos(ang), jnp.sin(ang)
    lane = jnp.arange(LANES) % group
    first = (lane >= start) & (lane < start + half)
    second = (lane >= start + half) & (lane < start + rot)
    idx = jnp.clip(jnp.where(second, lane - start - half, lane - start), 0, half - 1)
    cos_t = jnp.where(first | second, cos[:, idx], 1.0)
    sin_up = jnp.where(first, -sin[:, idx], 0.0)
    sin_dn = jnp.where(second, sin[:, idx], 0.0)
    return cos_t, sin_up, sin_dn


def _swa_qkv_kernel(x_ref, g_ref, w_ref, qg_ref, kg_ref, c_ref, su_ref, sd_ref, q_ref, k_ref, v_ref):
    hn = (_rms(x_ref[...]) * g_ref[...]).astype(BF16)
    nq = q_ref.shape[1]
    nk = k_ref.shape[1]
    cos_t, sin_up, sin_dn = c_ref[...], su_ref[...], sd_ref[...]
    r2 = lax.broadcasted_iota(jnp.int32, (LANES, LANES), 0) // HEAD_DIM
    c2 = lax.broadcasted_iota(jnp.int32, (LANES, LANES), 1) // HEAD_DIM
    head_ones = jnp.where(r2 == c2, 1.0, 0.0).astype(BF16)

    def head_norm_rope(xc, gain):
        ss = _dot((xc * xc).astype(BF16), head_ones)
        xn = xc * lax.rsqrt(ss * (1.0 / HEAD_DIM) + NORM_EPS) * gain
        return _rope_apply(xn, cos_t, sin_up, sin_dn, SWA_ROT // 2)

    width = nk
    proj = lambda c0: _dot(hn, w_ref[:, c0:c0 + width])
    groups = [(q_ref, c0, qg_ref) for c0 in range(0, nq, width)] + [(k_ref, 0, kg_ref)]

    def epilogue(y, out_ref, c0, gain_ref):
        for c in range(width // LANES):
            sl = slice(c * LANES, (c + 1) * LANES)
            out_ref[:, c0 + c * LANES:c0 + (c + 1) * LANES] = head_norm_rope(y[:, sl], gain_ref[...]).astype(BF16)

    y = proj(0)
    for gi, (out_ref, c0, gain_ref) in enumerate(groups):
        y_next = proj((gi + 1) * width)
        epilogue(y, out_ref, c0, gain_ref)
        y = y_next
    v_ref[...] = y.astype(BF16)


def _swa_attn_kernel(sink_ref, q_ref, k_ref, v_ref, wo_ref, x_ref, o_ref, att_ref):
    i = pl.program_id(1)
    s = k_ref.shape[1]
    blk = q_ref.shape[1]
    band = 3 * blk
    start = pl.multiple_of(jnp.clip((i - 1) * blk, 0, s - band), blk)
    group = SWA_HEADS // SWA_KV_HEADS
    rows = group * blk
    qpos = i * blk + lax.broadcasted_iota(jnp.int32, (rows, band), 0) % blk
    kpos = start + lax.broadcasted_iota(jnp.int32, (rows, band), 1)
    mask = jnp.abs(qpos - kpos) <= SWA_WINDOW
    m0q = _low_half((blk, LANES))
    rowblk = lax.broadcasted_iota(jnp.int32, (rows, 1), 0) // blk
    zq = jnp.zeros((blk, LANES), BF16)
    kv_groups = range(SWA_KV_HEADS)

    def stacked_q(g):
        parts = []
        for p in range(group // 2):
            qp = q_ref[0, :, (g * (group // 2) + p) * LANES:(g * (group // 2) + p + 1) * LANES]
            parts += [jnp.where(m0q, qp, zq), jnp.where(m0q, zq, qp)]
        return jnp.concatenate(parts, axis=0)

    def sink_col(g):
        sink = jnp.zeros((rows, 1), F32)
        for jh in range(group):
            sink = jnp.where(rowblk == jh, sink_ref[g * group + jh], sink)
        return sink

    band_of = lambda ref, g: ref[0, pl.ds(start, band), g * LANES:(g + 1) * LANES]
    sc = [jnp.where(mask, _dot(stacked_q(g), band_of(k_ref, g), NT), NEG_INF) for g in kv_groups]
    sinks = [sink_col(g) for g in kv_groups]
    mx = [jnp.maximum(jnp.max(s_, axis=-1, keepdims=True), sk) for s_, sk in zip(sc, sinks)]
    pe = [jnp.exp(s_ - m_) for s_, m_ in zip(sc, mx)]
    den = [jnp.sum(p_, axis=-1, keepdims=True) + jnp.exp(sk - m_) for p_, sk, m_ in zip(pe, sinks, mx)]
    ob = [_dot(p_.astype(BF16), band_of(v_ref, g)) / d_ for p_, g, d_ in zip(pe, kv_groups, den)]
    for g in kv_groups:
        for p in range(group // 2):
            o0 = ob[g][(2 * p) * blk:(2 * p + 1) * blk]
            o1 = ob[g][(2 * p + 1) * blk:(2 * p + 2) * blk]
            col = (g * (group // 2) + p) * LANES
            att_ref[:, col:col + LANES] = jnp.where(m0q, o0, o1).astype(BF16)
    o_ref[0] = x_ref[0] + _dot(att_ref[...], wo_ref[...])


def _swa_layer(x, g, w_qkv, q_gain, k_gain, sinks, w_o):
    b, s, d = x.shape
    t = b * s
    qd = SWA_HEADS * HEAD_DIM
    kd = SWA_KV_HEADS * HEAD_DIM
    wq = w_qkv[:, :qd]
    wk = w_qkv[:, qd:qd + kd].reshape(d, SWA_KV_HEADS, 1, HEAD_DIM)
    wv = w_qkv[:, qd + kd:].reshape(d, SWA_KV_HEADS, 1, HEAD_DIM)
    dup = lambda w: jnp.broadcast_to(w, (d, SWA_KV_HEADS, 2, HEAD_DIM)).reshape(d, 2 * kd)
    w_all = jnp.concatenate([wq, dup(wk), dup(wv)], axis=1).astype(BF16)
    nk = 2 * kd
    qg = (jnp.tile(q_gain, 2) * HEAD_DIM ** -0.5).reshape(1, LANES)
    kg = jnp.tile(k_gain, 2).reshape(1, LANES)
    cos_t, sin_up, sin_dn = _rope_tables(s, SWA_ROT, 0, HEAD_DIM)
    tm = _tiles(t)["proj"]
    tm = min(tm, s)
    nt = s // tm
    tab = pl.BlockSpec((tm, LANES), lambda i: (i % nt, 0))
    q, k, v = pl.pallas_call(
        _swa_qkv_kernel,
        grid=(t // tm,),
        in_specs=[
            pl.BlockSpec((tm, d), lambda i: (i, 0)),
            pl.BlockSpec((1, d), lambda i: (0, 0)),
            pl.BlockSpec((d, qd + 2 * nk), lambda i: (0, 0)),
            pl.BlockSpec((1, LANES), lambda i: (0, 0)),
            pl.BlockSpec((1, LANES), lambda i: (0, 0)),
            tab, tab, tab,
        ],
        out_specs=[
            pl.BlockSpec((tm, qd), lambda i: (i, 0)),
            pl.BlockSpec((tm, nk), lambda i: (i, 0)),
            pl.BlockSpec((tm, nk), lambda i: (i, 0)),
        ],
        out_shape=[jax.ShapeDtypeStruct((t, qd), BF16), jax.ShapeDtypeStruct((t, nk), BF16),
                   jax.ShapeDtypeStruct((t, nk), BF16)],
        compiler_params=_params("parallel"),
        name="swa_qkv",
    )(x.reshape(t, d), g.reshape(1, d), w_all, qg, kg, cos_t, sin_up, sin_dn)

    nb = s // SWA_BLOCK
    o = pl.pallas_call(
        _swa_attn_kernel,
        grid_spec=pltpu.PrefetchScalarGridSpec(
            num_scalar_prefetch=1,
            grid=(b, nb),
            in_specs=[
                pl.BlockSpec((1, SWA_BLOCK, qd), lambda bi, i, sk: (bi, i, 0)),
                pl.BlockSpec((1, s, nk), lambda bi, i, sk: (bi, 0, 0)),
                pl.BlockSpec((1, s, nk), lambda bi, i, sk: (bi, 0, 0)),
                pl.BlockSpec((qd, d), lambda bi, i, sk: (0, 0)),
                pl.BlockSpec((1, SWA_BLOCK, d), lambda bi, i, sk: (bi, i, 0)),
            ],
            out_specs=pl.BlockSpec((1, SWA_BLOCK, d), lambda bi, i, sk: (bi, i, 0)),
            scratch_shapes=[pltpu.VMEM((SWA_BLOCK, qd), BF16)],
        ),
        out_shape=jax.ShapeDtypeStruct((b, s, d), F32),
        compiler_params=_params("parallel", "arbitrary"),
        name="swa_attention",
    )(sinks.astype(F32), q.reshape(b, s, qd), k.reshape(b, s, nk), v.reshape(b, s, nk), w_o.astype(BF16), x)
    return o


def _mla_proj_kernel(x_ref, g_ref, wd_ref, cqg_ref, ckvg_ref, wuq_ref, wur_ref, wuk_ref, wuv_ref, kg_ref,
                     qc_ref, sa_ref, c_ref, su_ref, sd_ref, q_ref, k_ref, v_ref):
    hn = (_rms(x_ref[...]) * g_ref[...]).astype(BF16)
    down = _dot(hn, wd_ref[...])
    cq = (_rms(down[:, :MLA_Q_LORA]) * cqg_ref[...]).astype(BF16)
    ckv = (_rms(down[:, MLA_Q_LORA:MLA_Q_LORA + MLA_KV_LORA]) * ckvg_ref[...]).astype(BF16)
    tail = down[:, MLA_Q_LORA + MLA_KV_LORA:]
    k_rope = pltpu.roll(tail, MLA_NOPE, 1)
    qp = _dot(cq, wuq_ref[...])
    qr = _dot(cq, wur_ref[...])
    kp = _dot(ckv, wuk_ref[...])
    v_ref[...] = _dot(ckv, wuv_ref[...]).astype(BF16)
    inv_dim = 1.0 / (MLA_NOPE + MLA_ROPE)
    gain_cos, sin_abs, kg = qc_ref[...], sa_ref[...], kg_ref[...]
    kr = _rope_apply(k_rope * kg, c_ref[...], su_ref[...], sd_ref[...], MLA_ROPE // 2)

    def inv_rms(xh):
        return lax.rsqrt(jnp.sum(xh * xh, axis=-1, keepdims=True) * inv_dim + NORM_EPS)

    for h in range(MLA_HEADS):
        sl = slice(h * LANES, (h + 1) * LANES)
        xq = qp[:, sl]
        q_ref[:, sl] = ((xq * gain_cos + qr[:, sl] * sin_abs) * inv_rms(xq)).astype(BF16)
        xk = kp[:, sl]
        k_ref[:, sl] = ((xk * kg + kr) * inv_rms(xk + k_rope)).astype(BF16)


def _mla_attn_kernel(q_ref, k_ref, v_ref, o_ref, sc_ref, mx_ref):
    i = pl.program_id(2)
    s = k_ref.shape[1]
    kc = s // MLA_KEY_CHUNKS
    heads = range(2 * MLA_PAIRS_PER_STEP)
    chunks = range(MLA_KEY_CHUNKS)
    lanes = [slice(m * LANES, (m + 1) * LANES) for m in heads]
    vlanes = [slice((m // 2) * LANES, (m // 2 + 1) * LANES) for m in heads]

    @pl.when(i == 0)
    def _():
        sc_ref[...] = jnp.zeros(sc_ref.shape, F32)
        mx_ref[...] = jnp.zeros(mx_ref.shape, F32)

    rows = lambda c: slice(c * kc, (c + 1) * kc)
    den = [0.0 for _ in heads]
    ot = [0.0 for _ in heads]
    mx_old = [mx_ref[m] for m in heads]
    mx_new = [None for _ in heads]
    for c in chunks:
        for m in heads:
            pe = jnp.exp(sc_ref[m, rows(c), :] - mx_old[m])
            den[m] = den[m] + jnp.sum(pe, axis=0, keepdims=True)
            ot[m] = ot[m] + _dot(v_ref[0, rows(c), vlanes[m]], pe.astype(BF16), TN)
        for m in heads:
            sc = _dot(k_ref[0, rows(c), lanes[m]], q_ref[0, :, lanes[m]], NT)
            sc_ref[m, rows(c), :] = sc
            cmax = jnp.max(sc, axis=0, keepdims=True)
            mx_new[m] = cmax if c == 0 else jnp.maximum(mx_new[m], cmax)
    for m in heads:
        mx_ref[m] = mx_new[m]
    halves = [(ot[m] / den[m])[(m % 2) * MLA_V:(m % 2 + 1) * MLA_V] for m in heads]
    for p in range(MLA_PAIRS_PER_STEP):
        pair = jnp.concatenate(halves[2 * p:2 * p + 2], axis=0)
        o_ref[0, :, p * LANES:(p + 1) * LANES] = pair.T.astype(BF16)


def _mla_layer(x, g, w_down, cq_gain, ckv_gain, w_uq, w_ukv, q_gain, k_gain, w_o):
    b, s, d = x.shape
    t = b * s
    hd = MLA_NOPE + MLA_ROPE
    n_down = w_down.shape[1]
    wd = jnp.pad(w_down, ((0, 0), (0, MLA_DOWN_PAD - n_down))).astype(BF16)
    pad_heads = lambda w, width: jnp.pad(w, ((0, 0), (0, 0), (0, LANES - width))).reshape(w.shape[0], MLA_HEADS * LANES)
    wq3 = w_uq.reshape(MLA_Q_LORA, MLA_HEADS, hd)
    wuq = pad_heads(wq3, hd).astype(BF16)
    q_scale = q_gain * hd ** -0.5
    wrot = wq3[:, :, MLA_NOPE:] * q_scale[MLA_NOPE:]
    wrot = jnp.concatenate([-wrot[:, :, MLA_ROPE // 2:], wrot[:, :, :MLA_ROPE // 2]], axis=-1)
    wur = jnp.pad(wrot, ((0, 0), (0, 0), (MLA_NOPE, LANES - hd))).reshape(MLA_Q_LORA, MLA_HEADS * LANES).astype(BF16)
    wkv = w_ukv.reshape(MLA_KV_LORA, MLA_HEADS, MLA_NOPE + MLA_V)
    wuk = pad_heads(wkv[:, :, :MLA_NOPE], MLA_NOPE).astype(BF16)
    wuv = wkv[:, :, MLA_NOPE:].reshape(MLA_KV_LORA, MLA_HEADS * MLA_V).astype(BF16)
    qg = jnp.pad(q_scale, (0, LANES - hd)).reshape(1, LANES)
    kg = jnp.pad(k_gain, (0, LANES - hd)).reshape(1, LANES)
    cos_t, sin_up, sin_dn = _rope_tables(s, MLA_ROPE, MLA_NOPE, LANES)
    gain_cos = cos_t * qg
    sin_abs = sin_dn - sin_up
    tiles = _tiles(t)
    tm = min(tiles["mla_proj"], s)
    nt = s // tm
    nq = MLA_HEADS * LANES
    nv = MLA_HEADS * MLA_V
    full = lambda shape: pl.BlockSpec(shape, lambda i: (0, 0))
    tab = pl.BlockSpec((tm, LANES), lambda i: (i % nt, 0))
    q, k, v = pl.pallas_call(
        _mla_proj_kernel,
        grid=(t // tm,),
        in_specs=[
            pl.BlockSpec((tm, d), lambda i: (i, 0)),
            full((1, d)), full((d, MLA_DOWN_PAD)), full((1, MLA_Q_LORA)), full((1, MLA_KV_LORA)),
            full((MLA_Q_LORA, nq)), full((MLA_Q_LORA, nq)), full((MLA_KV_LORA, nq)), full((MLA_KV_LORA, nv)),
            full((1, LANES)), tab, tab, tab, tab, tab,
        ],
        out_specs=[
            pl.BlockSpec((tm, nq), lambda i: (i, 0)),
            pl.BlockSpec((tm, nq), lambda i: (i, 0)),
            pl.BlockSpec((tm, nv), lambda i: (i, 0)),
        ],
        out_shape=[jax.ShapeDtypeStruct((t, nq), BF16), jax.ShapeDtypeStruct((t, nq), BF16),
                   jax.ShapeDtypeStruct((t, nv), BF16)],
        compiler_params=_params("parallel"),
        name="mla_proj",
    )(x.reshape(t, d), g.reshape(1, d), wd, cq_gain.reshape(1, -1), ckv_gain.reshape(1, -1), wuq, wur, wuk, wuv,
      kg, gain_cos, sin_abs, cos_t, sin_up, sin_dn)

    tq = min(tiles["mla_q"], s)
    n_q = s // tq
    pp = MLA_PAIRS_PER_STEP
    o = pl.pallas_call(
        _mla_attn_kernel,
        grid=(b, MLA_HEADS // (2 * pp), n_q + 1),
        in_specs=[
            pl.BlockSpec((1, tq, 2 * pp * LANES), lambda bi, p, i: (bi, jnp.minimum(i, n_q - 1), p)),
            pl.BlockSpec((1, s, 2 * pp * LANES), lambda bi, p, i: (bi, 0, p)),
            pl.BlockSpec((1, s, pp * LANES), lambda bi, p, i: (bi, 0, p)),
        ],
        out_specs=pl.BlockSpec((1, tq, pp * LANES), lambda bi, p, i: (bi, jnp.maximum(i - 1, 0), p)),
        out_shape=jax.ShapeDtypeStruct((b, s, nv), BF16),
        scratch_shapes=[pltpu.VMEM((2 * pp, s, tq), F32), pltpu.VMEM((2 * pp, 1, tq), F32)],
        compiler_params=_params("parallel", "parallel", "arbitrary"),
        name="mla_attention",
    )(q.reshape(b, s, nq), k.reshape(b, s, nq), v.reshape(b, s, nv))
    return _proj_res(o.reshape(t, nv), w_o, x.reshape(t, d)).reshape(b, s, d)


def _rwkv_proj_kernel(x_ref, xp_ref, xn_ref, g_ref, mu_ref, wr_ref, wk_ref, wv_ref, g1_ref, g2_ref,
                      w1_ref, w2_ref, w0_ref, a1_ref, a2_ref, a0_ref,
                      r_ref, k_ref, v_ref, gt_ref, lwf_ref, lwb_ref, af_ref, ab_ref, *, tiles_per_seq):
    i = pl.program_id(0)
    gain = g_ref[...]
    h = _rms(x_ref[...]) * gain
    tm = h.shape[0]
    first = (i % tiles_per_seq) == 0
    last = (i % tiles_per_seq) == tiles_per_seq - 1
    h_before = jnp.where(first, 0.0, _rms(xp_ref[7:8, :]) * gain)
    h_after = jnp.where(last, 0.0, _rms(xn_ref[0:1, :]) * gain)
    row = lax.broadcasted_iota(jnp.int32, h.shape, 0)
    h_prev = jnp.where(row == 0, h_before, pltpu.roll(h, 1, 0))
    h_next = jnp.where(row == tm - 1, h_after, pltpu.roll(h, tm - 1, 0))
    xx = 0.5 * (h_prev + h_next) - h
    mix = lambda n: (h + xx * mu_ref[n:n + 1, :]).astype(BF16)
    r_ref[...] = _dot(mix(0), wr_ref[...]).astype(r_ref.dtype)
    k_ref[...] = _dot(mix(2), wk_ref[...]).astype(k_ref.dtype)
    v_ref[...] = _dot(mix(3), wv_ref[...]).astype(v_ref.dtype)
    gt_ref[...] = _dot(jax.nn.sigmoid(_dot(mix(5), g1_ref[...])).astype(BF16), g2_ref[...]).astype(gt_ref.dtype)
    d = h.shape[1]
    wl = w0_ref[...] + _dot(jnp.tanh(_dot(mix(1), w1_ref[...])).astype(BF16), w2_ref[...])
    lw = -jnp.exp(-0.5) * jax.nn.sigmoid(wl)
    lwf_ref[...] = lw[:, :d]
    lwb_ref[...] = lw[:, d:]
    al = jax.nn.sigmoid(a0_ref[...] + _dot(_dot(mix(4), a1_ref[...]).astype(BF16), a2_ref[...]))
    af_ref[...] = al[:, :d].astype(af_ref.dtype)
    ab_ref[...] = al[:, d:].astype(ab_ref.dtype)


def _wkv_masks(c, rev):
    n = 2 * WKV_PAIRS * c
    r = lax.broadcasted_iota(jnp.int32, (n, WKV_PAIRS * c), 0)
    q = lax.broadcasted_iota(jnp.int32, (n, WKV_PAIRS * c), 1)
    i, j = r % c, q % c
    same_pair = ((r // c) % WKV_PAIRS) == (q // c)
    upto = same_pair & ((j >= i) if rev else (j <= i))
    before = same_pair & ((j > i) if rev else (j < i))
    half = r // (WKV_PAIRS * c)
    rn = lax.broadcasted_iota(jnp.int32, (n, n), 0)
    cn = lax.broadcasted_iota(jnp.int32, (n, n), 1)
    ns = WKV_PAIRS * LANES
    r2 = lax.broadcasted_iota(jnp.int32, (ns, ns), 0)
    c2 = lax.broadcasted_iota(jnp.int32, (ns, ns), 1)
    ri = lax.broadcasted_iota(jnp.int32, (c, c), 0)
    ci = lax.broadcasted_iota(jnp.int32, (c, c), 1)
    sizes = []
    s = 2
    while s < c:
        sizes.append(s)
        s *= 2
    return {
        "upto": [upto & (half == m) for m in range(2)],
        "before": [before & (half == m) for m in range(2)],
        "join": [((rn // (2 * s)) == (cn // (2 * s))) & ((rn // s) != (cn // s)) for s in sizes],
        "base": (rn // 2) == (cn // 2),
        "eye": jnp.where(rn == cn, 1.0, 0.0),
        "diag": r2 == c2,
        "same_head": (r2 // HEAD_DIM) == (c2 // HEAD_DIM),
        "cum": jnp.where((ci >= ri) if rev else (ci <= ri), 1.0, 0.0),
    }


def _cat_l(parts):
    return jnp.concatenate(parts, axis=1)


def _cat_r(parts):
    return jnp.concatenate(parts, axis=0)


def _pair(t, p):
    return t[:, p * LANES:(p + 1) * LANES]


def _stack_pairs(t, reps):
    return _cat_r([_pair(t, p) for _ in range(reps) for p in range(WKV_PAIRS)])


def _wkv_prepare(r, lw, k, v, asig, k_k, k_a, rev, masks):
    c = r.shape[0]
    m0p = _low_half((c, LANES))
    pairs = range(WKV_PAIRS)
    pair, cat_l, cat_r = _pair, _cat_l, _cat_r

    kk = k * k_k
    kk2 = kk * kk
    kk = kk / jnp.maximum(jnp.sqrt(cat_l([_pair_sum(pair(kk2, p), m0p) for p in pairs])), 1e-12)
    bvec = kk * asig
    kd = k * (1.0 + (asig - 1.0) * k_a)

    cl = _mm_exact_lhs(masks["cum"], lw)
    tot = cl[0:1, :] if rev else cl[c - 1:c, :]
    g_inv = jnp.exp(-cl)
    g_end = jnp.exp(tot - cl)
    at = -kk * jnp.exp(cl - lw)
    rt = r * jnp.exp(cl)
    bt = bvec * g_inv
    kt = kd * g_inv
    b_end = bvec * g_end
    k_end = kd * g_end

    lane = lax.broadcasted_iota(jnp.int32, r.shape, 1)
    low = lane % LANES < HEAD_DIM
    lhs = cat_r([jnp.where(low, at, 0.0), jnp.where(low, 0.0, at), jnp.where(low, rt, 0.0), jnp.where(low, 0.0, rt)])
    rhs = cat_r([jnp.where(lane // LANES == p, t, 0.0) for t in (bt, kt) for p in pairs])
    sc = _mm(lhs, rhs, NT, WKV_SCORE_PASSES)
    pc = WKV_PAIRS * c

    def block_diag(lhs_kind, rhs_kind, key):
        s = sc[lhs_kind * 2 * c:(lhs_kind + 1) * 2 * c, rhs_kind * pc:(rhs_kind + 1) * pc]
        st = cat_r([s[m * c:(m + 1) * c] for m in range(2) for _ in pairs])
        return cat_l([jnp.where(masks[key][m], st, 0.0) for m in range(2)])

    return {"a_ab": block_diag(0, 0, "before"), "a_rb": block_diag(1, 0, "upto"),
            "a_ak": block_diag(0, 1, "before"), "a_rk": block_diag(1, 1, "upto"),
            "at": at, "rt": rt, "v": v, "b_end": b_end, "k_end": k_end, "decay": jnp.exp(tot)}


def _wkv_chunks(units):
    pres = [u[0] for u in units]
    states = [u[1] for u in units]
    masks = [u[2] for u in units]
    c = pres[0]["at"].shape[0]
    m0p = _low_half((c, LANES))
    inv = [mk["eye"] + jnp.where(mk["base"], p["a_ab"], 0.0) for p, mk in zip(pres, masks)]
    for lvl in range(len(masks[0]["join"])):
        ex = [_mm(jnp.where(mk["join"][lvl], p["a_ab"], 0.0), x, NN, WKV_INV_PASSES)
              for p, mk, x in zip(pres, masks, inv)]
        inv = [x + _mm(x, t, NN, WKV_INV_PASSES) for x, t in zip(inv, ex)]
    v_st = [_stack_pairs(p["v"], 2) for p in pres]
    x_ak = [_mm(p["a_ak"], vs) for p, vs in zip(pres, v_st)]
    yk = [_mm(p["a_rk"], vs) for p, vs in zip(pres, v_st)]
    zz = [_mm(x, _cat_l([_stack_pairs(p["at"], 2), xa]), NN, WKV_INV_PASSES)
          for x, p, xa in zip(inv, pres, x_ak)]

    def unstack(t):
        return _cat_l([jnp.where(m0p, t[p * c:(p + 1) * c], t[(WKV_PAIRS + p) * c:(WKV_PAIRS + p + 1) * c])
                       for p in range(WKV_PAIRS)])

    w_a = [unstack(z[:, :LANES]) for z in zz]
    u_0 = [unstack(z[:, LANES:]) for z in zz]
    hw = [_mm(_cat_r([w, p["rt"]]), st, NN, WKV_STATE_PASSES) for w, p, st in zip(w_a, pres, states)]
    u = [h[:c] + u0 for h, u0 in zip(hw, u_0)]
    yu = [_mm(p["a_rb"], _stack_pairs(uu, 2)) for p, uu in zip(pres, u)]
    grow = [_mm(_cat_r([p["b_end"], p["k_end"]]), _cat_r([uu, p["v"]]), TN) for p, uu in zip(pres, u)]
    out = []
    for p, mk, st, h, yuu, ykk, gr in zip(pres, masks, states, hw, yu, yk, grow):
        y = h[c:] + unstack(yuu + ykk)
        decay_col = jnp.sum(jnp.where(mk["diag"], p["decay"], 0.0), axis=-1, keepdims=True)
        out.append((y, st * decay_col + jnp.where(mk["same_head"], gr, 0.0)))
    return out


def _rwkv_scan_kernel(rf_ref, kf_ref, vf_ref, rb_ref, kb_ref, vb_ref, lwf_ref, af_ref, lwb_ref, ab_ref,
                      kk_ref, ka_ref, yf_ref, yb_ref, st_ref, *, groups):
    @pl.when(pl.program_id(2) == 0)
    def _():
        st_ref[...] = jnp.zeros_like(st_ref)

    c = rf_ref.shape[1]
    width = WKV_PAIRS * LANES
    dirs = ((rf_ref, kf_ref, vf_ref, lwf_ref, af_ref, False), (rb_ref, kb_ref, vb_ref, lwb_ref, ab_ref, True))
    units = []
    for di, (r_ref, k_ref, v_ref, lw_ref, a_ref, rev) in enumerate(dirs):
        masks = _wkv_masks(c, rev)
        for gi in range(groups):
            sl = slice(gi * width, (gi + 1) * width)
            f32 = lambda ref: ref[0, :, sl].astype(F32)
            pre = _wkv_prepare(f32(r_ref), lw_ref[0, :, sl], f32(k_ref), f32(v_ref), f32(a_ref),
                               kk_ref[:, sl], ka_ref[:, sl], rev, masks)
            units.append((pre, st_ref[di, gi], masks))
    results = _wkv_chunks(units)
    for ui, (y, st) in enumerate(results):
        di, gi = divmod(ui, groups)
        (yf_ref, yb_ref)[di][0, :, gi * width:(gi + 1) * width] = y.astype(yf_ref.dtype)
        st_ref[di, gi] = st


def _rwkv_out_kernel(x_ref, r_ref, k_ref, v_ref, af_ref, ab_ref, gt_ref, yf_ref, yb_ref,
                     ka_ref, rk_ref, lw_ref, lb_ref, wo_ref, o_ref, act_ref):
    inv_n = 1.0 / HEAD_DIM
    r2 = lax.broadcasted_iota(jnp.int32, (LANES, LANES), 0) // HEAD_DIM
    c2 = lax.broadcasted_iota(jnp.int32, (LANES, LANES), 1) // HEAD_DIM
    head_ones = jnp.where(r2 == c2, 1.0, 0.0).astype(BF16)

    def head_sum(t, passes):
        hi = t.astype(BF16)
        out = _dot(hi, head_ones)
        if passes == 2:
            out = out + _dot((t - hi.astype(F32)).astype(BF16), head_ones)
        return out

    for c in range(x_ref.shape[1] // LANES):
        sl = slice(c * LANES, (c + 1) * LANES)
        f32 = lambda ref: ref[:, sl].astype(F32)
        r, k, v = f32(r_ref), f32(k_ref), f32(v_ref)
        k_a = ka_ref[:, sl]
        kd_both = k * (2.0 + (f32(af_ref) + f32(ab_ref) - 2.0) * k_a)
        bonus = head_sum(r * rk_ref[:, sl] * kd_both, 2) * v
        y = f32(yf_ref) + f32(yb_ref)
        dev = y - head_sum(y, 2) * inv_n
        var = head_sum(dev * dev, 1) * inv_n
        yn = dev * lax.rsqrt(var + RWKV_GN_EPS) * lw_ref[:, sl] + lb_ref[:, sl]
        act_ref[:, sl] = ((yn + bonus) * gt_ref[:, sl]).astype(BF16)
    o_ref[...] = x_ref[...] + _dot(act_ref[...], wo_ref[...])


def _rwkv_layer(x, g, mu, w_r, w_k, w_v, w0, w1, w2, a0, a1, a2, g1, g2, k_k, k_a, r_k, lnx_w, lnx_b, w_o):
    b, s, d = x.shape
    t = b * s
    tiles = _tiles(t)
    tm = min(tiles["rwkv_proj"], s)
    xf = x.reshape(t, d)
    lora_g = g1.shape[1]
    gpad = -lora_g % LANES
    g1p = jnp.pad(g1, ((0, 0), (0, gpad))).astype(BF16)
    g2p = jnp.pad(g2, ((0, gpad), (0, 0))).astype(BF16)

    def both_dirs(m1, m2):
        l = m1.shape[2]
        z = jnp.zeros((l, d), m2.dtype)
        cat = jnp.concatenate([m1[0], m1[1]], axis=1)
        bd = jnp.concatenate([jnp.concatenate([m2[0], z], axis=1), jnp.concatenate([z, m2[1]], axis=1)], axis=0)
        return cat.astype(BF16), bd.astype(BF16)

    w1c, w2c = both_dirs(w1, w2)
    a1c, a2c = both_dirs(a1, a2)
    full = lambda arr: pl.BlockSpec(arr.shape, lambda i: (0,) * arr.ndim)
    halo = tm // 8
    n8 = t // 8
    tok = pl.BlockSpec((tm, d), lambda i: (i, 0))
    weights = [g.reshape(1, d), mu, w_r.astype(BF16), w_k.astype(BF16), w_v.astype(BF16), g1p, g2p,
               w1c, w2c, w0.reshape(1, 2 * d), a1c, a2c, a0.reshape(1, 2 * d)]
    outs = pl.pallas_call(
        functools.partial(_rwkv_proj_kernel, tiles_per_seq=s // tm),
        grid=(t // tm,),
        in_specs=[tok,
                  pl.BlockSpec((8, d), lambda i: (jnp.maximum(i * halo - 1, 0), 0)),
                  pl.BlockSpec((8, d), lambda i: (jnp.minimum((i + 1) * halo, n8 - 1), 0)),
                  ] + [full(w) for w in weights],
        out_specs=[tok] * 8,
        out_shape=[jax.ShapeDtypeStruct((t, d), dt) for dt in (BF16, BF16, BF16, BF16, F32, F32, BF16, BF16)],
        compiler_params=_params("parallel"),
        name="rwkv_proj",
    )(xf, xf, xf, *weights)
    r, k, v, gate, lw_f, lw_b, a_f, a_b = [o.reshape(b, s, d) for o in outs]

    c = RWKV_CHUNK
    nc = s // c
    groups = WKV_GROUPS_PER_STEP
    wpair = groups * WKV_PAIRS * LANES
    n_state = 2 * WKV_PAIRS * HEAD_DIM
    fwd = pl.BlockSpec((1, c, wpair), lambda bi, p, ci: (bi, ci, p))
    bwd = pl.BlockSpec((1, c, wpair), lambda bi, p, ci: (bi, nc - 1 - ci, p))
    vec = pl.BlockSpec((1, wpair), lambda bi, p, ci: (0, p))
    y_f, y_b = pl.pallas_call(
        functools.partial(_rwkv_scan_kernel, groups=groups),
        grid=(b, d // wpair, nc),
        in_specs=[fwd, fwd, fwd, bwd, bwd, bwd, fwd, fwd, bwd, bwd, vec, vec],
        out_specs=[fwd, bwd],
        out_shape=[jax.ShapeDtypeStruct((b, s, d), BF16)] * 2,
        scratch_shapes=[pltpu.VMEM((2, groups, n_state, n_state), F32)],
        compiler_params=_params("parallel", "parallel", "arbitrary"),
        name="rwkv_scan",
    )(r, k, v, r, k, v, lw_f, a_f, lw_b, a_b, k_k.reshape(1, d), k_a.reshape(1, d))

    flat = lambda z: z.reshape(t, d)
    rowv = lambda z: z.reshape(1, d)
    tmo = tiles["rwkv_proj"]
    tok_o = pl.BlockSpec((tmo, d), lambda i: (i, 0))
    vec_o = pl.BlockSpec((1, d), lambda i: (0, 0))
    out = pl.pallas_call(
        _rwkv_out_kernel,
        grid=(t // tmo,),
        in_specs=[tok_o] * 9 + [vec_o] * 4 + [pl.BlockSpec((d, d), lambda i: (0, 0))],
        out_specs=tok_o,
        out_shape=jax.ShapeDtypeStruct((t, d), F32),
        scratch_shapes=[pltpu.VMEM((tmo, d), BF16)],
        compiler_params=_params("parallel"),
        name="rwkv_out",
    )(xf, flat(r), flat(k), flat(v), flat(a_f), flat(a_b), flat(gate), flat(y_f), flat(y_b),
      rowv(k_a), rowv(r_k), rowv(lnx_w), rowv(lnx_b), w_o.astype(BF16))
    return out.reshape(b, s, d)


def kernel(x, norm_tok, norm_ch, ffn_w_up, ffn_conv_w, ffn_conv_b, ffn_w_down, swa_w_qkv, swa_q_gain, swa_k_gain, swa_sinks, swa_w_o, rwkv_mu, rwkv_w_r, rwkv_w_k, rwkv_w_v, rwkv_w0, rwkv_w1, rwkv_w2, rwkv_a0, rwkv_a1, rwkv_a2, rwkv_g1, rwkv_g2, rwkv_k_k, rwkv_k_a, rwkv_r_k, rwkv_lnx_w, rwkv_lnx_b, rwkv_w_o, mla_w_down, mla_cq_gain, mla_ckv_gain, mla_w_uq, mla_w_ukv, mla_q_gain, mla_k_gain, mla_w_o):
    depth = norm_tok.shape[0]
    for i in range(depth):
        kind = i % N_MIXERS
        j = i // N_MIXERS
        if kind == 0:
            x = _swa_layer(x, norm_tok[i], swa_w_qkv[j], swa_q_gain[j], swa_k_gain[j], swa_sinks[j], swa_w_o[j])
        elif kind == 1:
            x = _rwkv_layer(x, norm_tok[i], rwkv_mu[j], rwkv_w_r[j], rwkv_w_k[j], rwkv_w_v[j], rwkv_w0[j],
                            rwkv_w1[j], rwkv_w2[j], rwkv_a0[j], rwkv_a1[j], rwkv_a2[j], rwkv_g1[j], rwkv_g2[j],
                            rwkv_k_k[j], rwkv_k_a[j], rwkv_r_k[j], rwkv_lnx_w[j], rwkv_lnx_b[j], rwkv_w_o[j])
        else:
            x = _mla_layer(x, norm_tok[i], mla_w_down[j], mla_cq_gain[j], mla_ckv_gain[j], mla_w_uq[j],
                           mla_w_ukv[j], mla_q_gain[j], mla_k_gain[j], mla_w_o[j])
        x = _ffn(x, norm_ch[i], ffn_w_up[i], ffn_conv_w[i], ffn_conv_b[i], ffn_w_down[i])
    return x
```

```python
import functools

import jax
import jax.numpy as jnp
from jax import lax
from jax.experimental import pallas as pl
from jax.experimental.pallas import tpu as pltpu

F32 = jnp.float32
BF16 = jnp.bfloat16

N_MIXERS = 3
ROPE_THETA = 500000.0
NORM_EPS = 1e-6
NEG_INF = -1e30
LANES = 128
HEAD_DIM = 64
SWA_HEADS = 16
SWA_KV_HEADS = 4
SWA_WINDOW = 128
SWA_BLOCK = 128
SWA_ROT = HEAD_DIM // 4
FFN_PAD_ROWS = 16
RWKV_GN_EPS = 64e-5
RWKV_CHUNK = 64
WKV_PAIRS = 2
WKV_GROUPS_PER_STEP = 4
WKV_SCORE_PASSES = 1
WKV_INV_PASSES = 1
WKV_STATE_PASSES = 1
MLA_HEADS = 16
MLA_NOPE = 64
MLA_ROPE = 32
MLA_V = 64
MLA_Q_LORA = 384
MLA_KV_LORA = 256
MLA_DOWN_PAD = 768
MLA_KEY_CHUNKS = 4
MLA_PAIRS_PER_STEP = 1
VMEM_LIMIT_BYTES = 56 * 1024 * 1024

NN = ((1,), (0,))
NT = ((1,), (1,))
TN = ((0,), (0,))


def _dot(a, b, dims=NN):
    return lax.dot_general(a, b, (dims, ((), ())), preferred_element_type=F32)


def _split3(x):
    hi = x.astype(BF16)
    r1 = x - hi.astype(F32)
    mid = r1.astype(BF16)
    lo = (r1 - mid.astype(F32)).astype(BF16)
    return hi, mid, lo


def _mm(a, b, dims=NN, passes=1):
    if passes == 1:
        return _dot(a.astype(BF16), b.astype(BF16), dims)
    ah, am, _ = _split3(a)
    bh, bm, _ = _split3(b)
    return _dot(ah, bh, dims) + (_dot(ah, bm, dims) + _dot(am, bh, dims))


def _mm_exact_lhs(a01, b):
    bh, bm, bl = _split3(b)
    a = a01.astype(BF16)
    return _dot(a, bh) + (_dot(a, bm) + _dot(a, bl))


def _rms(x, eps=NORM_EPS):
    return x * lax.rsqrt(jnp.mean(x * x, axis=-1, keepdims=True) + eps)


def _low_half(shape):
    return lax.broadcasted_iota(jnp.int32, shape, 1) < HEAD_DIM


def _pair_sum(x, m0):
    s0 = jnp.sum(jnp.where(m0, x, 0.0), axis=-1, keepdims=True)
    s1 = jnp.sum(jnp.where(m0, 0.0, x), axis=-1, keepdims=True)
    return jnp.where(m0, s0, s1)


def _params(*sem):
    return pltpu.CompilerParams(dimension_semantics=sem, vmem_limit_bytes=VMEM_LIMIT_BYTES)


def _tiles(rows):
    def pick(pref):
        t = pref
        while rows % t:
            t //= 2
        return t
    return {"proj": pick(1024), "rwkv_proj": pick(512), "mla_proj": pick(512), "mla_q": pick(256)}


def _ffn_kernel(x_ref, g_ref, wg_ref, wv_ref, cwg_ref, cwv_ref, cbg_ref, cbv_ref, wd_ref, o_ref, hn_ref):
    j = pl.program_id(1)
    s = x_ref.shape[1]
    rows = hn_ref.shape[0]

    @pl.when(j == 0)
    def _():
        x = x_ref[0]
        hn_ref[0:s, :] = (_rms(x) * g_ref[...]).astype(BF16)
        hn_ref[s:rows, :] = jnp.zeros((rows - s, hn_ref.shape[1]), BF16)
        o_ref[0] = x

    hn = hn_ref[...]

    def conv(w_ref, cw_ref, cb_ref):
        u = _dot(hn, w_ref[...].astype(BF16))
        prev = pltpu.roll(u, 1, 0)
        nxt = pltpu.roll(u, rows - 1, 0)
        return prev * cw_ref[0:1, :] + cb_ref[...] + u * cw_ref[1:2, :] + nxt * cw_ref[2:3, :]

    gate = conv(wg_ref, cwg_ref, cbg_ref)
    val = conv(wv_ref, cwv_ref, cbv_ref)
    act = (gate * jax.nn.sigmoid(gate) * val).astype(BF16)
    o_ref[0] += _dot(act[0:s], wd_ref[...].astype(BF16))


def _ffn(x, g, w_up, conv_w, conv_b, w_down, tf=256):
    b, s, d = x.shape
    f = w_down.shape[0]
    nf = f // tf
    wu, wd = w_up, w_down
    cb = conv_b.reshape(1, 2 * f)
    return pl.pallas_call(
        _ffn_kernel,
        grid=(b, nf),
        in_specs=[
            pl.BlockSpec((1, s, d), lambda i, j: (i, 0, 0)),
            pl.BlockSpec((1, d), lambda i, j: (0, 0)),
            pl.BlockSpec((d, tf), lambda i, j: (0, j)),
            pl.BlockSpec((d, tf), lambda i, j: (0, nf + j)),
            pl.BlockSpec((3, tf), lambda i, j: (0, j)),
            pl.BlockSpec((3, tf), lambda i, j: (0, nf + j)),
            pl.BlockSpec((1, tf), lambda i, j: (0, j)),
            pl.BlockSpec((1, tf), lambda i, j: (0, nf + j)),
            pl.BlockSpec((tf, d), lambda i, j: (j, 0)),
        ],
        out_specs=pl.BlockSpec((1, s, d), lambda i, j: (i, 0, 0)),
        out_shape=jax.ShapeDtypeStruct((b, s, d), F32),
        scratch_shapes=[pltpu.VMEM((s + FFN_PAD_ROWS, d), BF16)],
        compiler_params=_params("parallel", "arbitrary"),
        name="conv_ffn",
    )(x, g.reshape(1, d), wu, wu, conv_w, conv_w, cb, cb, wd)


def _proj_res_kernel(a_ref, w_ref, x_ref, o_ref):
    o_ref[...] = x_ref[...] + _dot(a_ref[...], w_ref[...])


def _proj_res(a, w, x):
    t, k = a.shape
    d = w.shape[1]
    tm = _tiles(t)["proj"]
    return pl.pallas_call(
        _proj_res_kernel,
        grid=(t // tm,),
        in_specs=[
            pl.BlockSpec((tm, k), lambda i: (i, 0)),
            pl.BlockSpec((k, d), lambda i: (0, 0)),
            pl.BlockSpec((tm, d), lambda i: (i, 0)),
        ],
        out_specs=pl.BlockSpec((tm, d), lambda i: (i, 0)),
        out_shape=jax.ShapeDtypeStruct((t, d), F32),
        compiler_params=_params("parallel"),
        name="proj_residual",
    )(a, w.astype(BF16), x)


def _rope_apply(xn, cos_t, sin_up, sin_dn, half):
    return xn * cos_t + pltpu.roll(xn, LANES - half, 1) * sin_up + pltpu.roll(xn, half, 1) * sin_dn


def _rope_tables(s, rot, start, group):
    half = rot // 2
    inv = ROPE_THETA ** (-jnp.arange(0, rot, 2, dtype=F32) / rot)
    ang = jnp.arange(s, dtype=F32)[:, None] * inv[None, :]
    cos, sin = jnp.cos(ang), jnp.sin(ang)
    lane = jnp.arange(LANES) % group
    first = (lane >= start) & (lane < start + half)
    second = (lane >= start + half) & (lane < start + rot)
    idx = jnp.clip(jnp.where(second, lane - start - half, lane - start), 0, half - 1)
    cos_t = jnp.where(first | second, cos[:, idx], 1.0)
    sin_up = jnp.where(first, -sin[:, idx], 0.0)
    sin_dn = jnp.where(second, sin[:, idx], 0.0)
    return cos_t, sin_up, sin_dn


def _swa_qkv_kernel(x_ref, g_ref, w_ref, qg_ref, kg_ref, c_ref, su_ref, sd_ref, q_ref, k_ref, v_ref):
    hn = (_rms(x_ref[...]) * g_ref[...]).astype(BF16)
    nq = q_ref.shape[1]
    nk = k_ref.shape[1]
    cos_t, sin_up, sin_dn = c_ref[...], su_ref[...], sd_ref[...]
    r2 = lax.broadcasted_iota(jnp.int32, (LANES, LANES), 0) // HEAD_DIM
    c2 = lax.broadcasted_iota(jnp.int32, (LANES, LANES), 1) // HEAD_DIM
    head_ones = jnp.where(r2 == c2, 1.0, 0.0).astype(BF16)

    def head_norm_rope(xc, gain):
        ss = _dot((xc * xc).astype(BF16), head_ones)
        xn = xc * lax.rsqrt(ss * (1.0 / HEAD_DIM) + NORM_EPS) * gain
        return _rope_apply(xn, cos_t, sin_up, sin_dn, SWA_ROT // 2)

    width = nk
    proj = lambda c0: _dot(hn, w_ref[:, c0:c0 + width])
    groups = [(q_ref, c0, qg_ref) for c0 in range(0, nq, width)] + [(k_ref, 0, kg_ref)]

    def epilogue(y, out_ref, c0, gain_ref):
        for c in range(width // LANES):
            sl = slice(c * LANES, (c + 1) * LANES)
            out_ref[:, c0 + c * LANES:c0 + (c + 1) * LANES] = head_norm_rope(y[:, sl], gain_ref[...]).astype(BF16)

    y = proj(0)
    for gi, (out_ref, c0, gain_ref) in enumerate(groups):
        y_next = proj((gi + 1) * width)
        epilogue(y, out_ref, c0, gain_ref)
        y = y_next
    v_ref[...] = y.astype(BF16)


def _swa_attn_kernel(sink_ref, q_ref, k_ref, v_ref, wo_ref, x_ref, o_ref, att_ref):
    i = pl.program_id(1)
    s = k_ref.shape[1]
    blk = q_ref.shape[1]
    band = 3 * blk
    start = pl.multiple_of(jnp.clip((i - 1) * blk, 0, s - band), blk)
    group = SWA_HEADS // SWA_KV_HEADS
    rows = group * blk
    qpos = i * blk + lax.broadcasted_iota(jnp.int32, (rows, band), 0) % blk
    kpos = start + lax.broadcasted_iota(jnp.int32, (rows, band), 1)
    mask = jnp.abs(qpos - kpos) <= SWA_WINDOW
    m0q = _low_half((blk, LANES))
    rowblk = lax.broadcasted_iota(jnp.int32, (rows, 1), 0) // blk
    zq = jnp.zeros((blk, LANES), BF16)
    kv_groups = range(SWA_KV_HEADS)

    def stacked_q(g):
        parts = []
        for p in range(group // 2):
            qp = q_ref[0, :, (g * (group // 2) + p) * LANES:(g * (group // 2) + p + 1) * LANES]
            parts += [jnp.where(m0q, qp, zq), jnp.where(m0q, zq, qp)]
        return jnp.concatenate(parts, axis=0)

    def sink_col(g):
        sink = jnp.zeros((rows, 1), F32)
        for jh in range(group):
            sink = jnp.where(rowblk == jh, sink_ref[g * group + jh], sink)
        return sink

    band_of = lambda ref, g: ref[0, pl.ds(start, band), g * LANES:(g + 1) * LANES]
    sc = [jnp.where(mask, _dot(stacked_q(g), band_of(k_ref, g), NT), NEG_INF) for g in kv_groups]
    sinks = [sink_col(g) for g in kv_groups]
    mx = [jnp.maximum(jnp.max(s_, axis=-1, keepdims=True), sk) for s_, sk in zip(sc, sinks)]
    pe = [jnp.exp(s_ - m_) for s_, m_ in zip(sc, mx)]
    den = [jnp.sum(p_, axis=-1, keepdims=True) + jnp.exp(sk - m_) for p_, sk, m_ in zip(pe, sinks, mx)]
    ob = [_dot(p_.astype(BF16), band_of(v_ref, g)) / d_ for p_, g, d_ in zip(pe, kv_groups, den)]
    for g in kv_groups:
        for p in range(group // 2):
            o0 = ob[g][(2 * p) * blk:(2 * p + 1) * blk]
            o1 = ob[g][(2 * p + 1) * blk:(2 * p + 2) * blk]
            col = (g * (group // 2) + p) * LANES
            att_ref[:, col:col + LANES] = jnp.where(m0q, o0, o1).astype(BF16)
    o_ref[0] = x_ref[0] + _dot(att_ref[...], wo_ref[...])


def _swa_layer(x, g, w_qkv, q_gain, k_gain, sinks, w_o):
    b, s, d = x.shape
    t = b * s
    qd = SWA_HEADS * HEAD_DIM
    kd = SWA_KV_HEADS * HEAD_DIM
    wq = w_qkv[:, :qd]
    wk = w_qkv[:, qd:qd + kd].reshape(d, SWA_KV_HEADS, 1, HEAD_DIM)
    wv = w_qkv[:, qd + kd:].reshape(d, SWA_KV_HEADS, 1, HEAD_DIM)
    dup = lambda w: jnp.broadcast_to(w, (d, SWA_KV_HEADS, 2, HEAD_DIM)).reshape(d, 2 * kd)
    w_all = jnp.concatenate([wq, dup(wk), dup(wv)], axis=1).astype(BF16)
    nk = 2 * kd
    qg = (jnp.tile(q_gain, 2) * HEAD_DIM ** -0.5).reshape(1, LANES)
    kg = jnp.tile(k_gain, 2).reshape(1, LANES)
    cos_t, sin_up, sin_dn = _rope_tables(s, SWA_ROT, 0, HEAD_DIM)
    tm = _tiles(t)["proj"]
    tm = min(tm, s)
    nt = s // tm
    tab = pl.BlockSpec((tm, LANES), lambda i: (i % nt, 0))
    q, k, v = pl.pallas_call(
        _swa_qkv_kernel,
        grid=(t // tm,),
        in_specs=[
            pl.BlockSpec((tm, d), lambda i: (i, 0)),
            pl.BlockSpec((1, d), lambda i: (0, 0)),
            pl.BlockSpec((d, qd + 2 * nk), lambda i: (0, 0)),
            pl.BlockSpec((1, LANES), lambda i: (0, 0)),
            pl.BlockSpec((1, LANES), lambda i: (0, 0)),
            tab, tab, tab,
        ],
        out_specs=[
            pl.BlockSpec((tm, qd), lambda i: (i, 0)),
            pl.BlockSpec((tm, nk), lambda i: (i, 0)),
            pl.BlockSpec((tm, nk), lambda i: (i, 0)),
        ],
        out_shape=[jax.ShapeDtypeStruct((t, qd), BF16), jax.ShapeDtypeStruct((t, nk), BF16),
                   jax.ShapeDtypeStruct((t, nk), BF16)],
        compiler_params=_params("parallel"),
        name="swa_qkv",
    )(x.reshape(t, d), g.reshape(1, d), w_all, qg, kg, cos_t, sin_up, sin_dn)

    nb = s // SWA_BLOCK
    o = pl.pallas_call(
        _swa_attn_kernel,
        grid_spec=pltpu.PrefetchScalarGridSpec(
            num_scalar_prefetch=1,
            grid=(b, nb),
            in_specs=[
                pl.BlockSpec((1, SWA_BLOCK, qd), lambda bi, i, sk: (bi, i, 0)),
                pl.BlockSpec((1, s, nk), lambda bi, i, sk: (bi, 0, 0)),
                pl.BlockSpec((1, s, nk), lambda bi, i, sk: (bi, 0, 0)),
                pl.BlockSpec((qd, d), lambda bi, i, sk: (0, 0)),
                pl.BlockSpec((1, SWA_BLOCK, d), lambda bi, i, sk: (bi, i, 0)),
            ],
            out_specs=pl.BlockSpec((1, SWA_BLOCK, d), lambda bi, i, sk: (bi, i, 0)),
            scratch_shapes=[pltpu.VMEM((SWA_BLOCK, qd), BF16)],
        ),
        out_shape=jax.ShapeDtypeStruct((b, s, d), F32),
        compiler_params=_params("parallel", "arbitrary"),
        name="swa_attention",
    )(sinks.astype(F32), q.reshape(b, s, qd), k.reshape(b, s, nk), v.reshape(b, s, nk), w_o.astype(BF16), x)
    return o


def _mla_proj_kernel(x_ref, g_ref, wd_ref, cqg_ref, ckvg_ref, wuq_ref, wur_ref, wuk_ref, wuv_ref, kg_ref,
                     qc_ref, sa_ref, c_ref, su_ref, sd_ref, q_ref, k_ref, v_ref):
    hn = (_rms(x_ref[...]) * g_ref[...]).astype(BF16)
    down = _dot(hn, wd_ref[...])
    cq = (_rms(down[:, :MLA_Q_LORA]) * cqg_ref[...]).astype(BF16)
    ckv = (_rms(down[:, MLA_Q_LORA:MLA_Q_LORA + MLA_KV_LORA]) * ckvg_ref[...]).astype(BF16)
    tail = down[:, MLA_Q_LORA + MLA_KV_LORA:]
    k_rope = pltpu.roll(tail, MLA_NOPE, 1)
    qp = _dot(cq, wuq_ref[...])
    qr = _dot(cq, wur_ref[...])
    kp = _dot(ckv, wuk_ref[...])
    v_ref[...] = _dot(ckv, wuv_ref[...]).astype(BF16)
    inv_dim = 1.0 / (MLA_NOPE + MLA_ROPE)
    gain_cos, sin_abs, kg = qc_ref[...], sa_ref[...], kg_ref[...]
    kr = _rope_apply(k_rope * kg, c_ref[...], su_ref[...], sd_ref[...], MLA_ROPE // 2)

    def inv_rms(xh):
        return lax.rsqrt(jnp.sum(xh * xh, axis=-1, keepdims=True) * inv_dim + NORM_EPS)

    for h in range(MLA_HEADS):
        sl = slice(h * LANES, (h + 1) * LANES)
        xq = qp[:, sl]
        q_ref[:, sl] = ((xq * gain_cos + qr[:, sl] * sin_abs) * inv_rms(xq)).astype(BF16)
        xk = kp[:, sl]
        k_ref[:, sl] = ((xk * kg + kr) * inv_rms(xk + k_rope)).astype(BF16)


def _mla_attn_kernel(q_ref, k_ref, v_ref, o_ref, sc_ref, mx_ref):
    i = pl.program_id(2)
    last = pl.num_programs(2) - 1
    s = k_ref.shape[1]
    kc = s // MLA_KEY_CHUNKS
    heads = range(2 * MLA_PAIRS_PER_STEP)
    chunks = range(MLA_KEY_CHUNKS)
    lanes = [slice(m * LANES, (m + 1) * LANES) for m in heads]
    vlanes = [slice((m // 2) * LANES, (m // 2 + 1) * LANES) for m in heads]
    rows = lambda c: slice(c * kc, (c + 1) * kc)

    def body(score, finish):
        den = [0.0 for _ in heads]
        ot = [0.0 for _ in heads]
        mx_old = [mx_ref[m] for m in heads] if finish else None
        mx_new = [None for _ in heads]
        for c in chunks:
            if finish:
                for m in heads:
                    pe = jnp.exp(sc_ref[m, rows(c), :] - mx_old[m])
                    den[m] = den[m] + jnp.sum(pe, axis=0, keepdims=True)
                    ot[m] = ot[m] + _dot(v_ref[0, rows(c), vlanes[m]], pe.astype(BF16), TN)
            if score:
                for m in heads:
                    sc = _dot(k_ref[0, rows(c), lanes[m]], q_ref[0, :, lanes[m]], NT)
                    sc_ref[m, rows(c), :] = sc
                    cmax = jnp.max(sc, axis=0, keepdims=True)
                    mx_new[m] = cmax if c == 0 else jnp.maximum(mx_new[m], cmax)
        if score:
            for m in heads:
                mx_ref[m] = mx_new[m]
        if finish:
            halves = [(ot[m] / den[m])[(m % 2) * MLA_V:(m % 2 + 1) * MLA_V] for m in heads]
            for p in range(MLA_PAIRS_PER_STEP):
                pair = jnp.concatenate(halves[2 * p:2 * p + 2], axis=0)
                o_ref[0, :, p * LANES:(p + 1) * LANES] = pair.T.astype(BF16)

    @pl.when(i == 0)
    def _():
        o_ref[...] = jnp.zeros(o_ref.shape, o_ref.dtype)
        body(True, False)

    pl.when(jnp.logical_and(i > 0, i < last))(lambda: body(True, True))
    pl.when(i == last)(lambda: body(False, True))


def _mla_layer(x, g, w_down, cq_gain, ckv_gain, w_uq, w_ukv, q_gain, k_gain, w_o):
    b, s, d = x.shape
    t = b * s
    hd = MLA_NOPE + MLA_ROPE
    n_down = w_down.shape[1]
    wd = jnp.pad(w_down, ((0, 0), (0, MLA_DOWN_PAD - n_down))).astype(BF16)
    pad_heads = lambda w, width: jnp.pad(w, ((0, 0), (0, 0), (0, LANES - width))).reshape(w.shape[0], MLA_HEADS * LANES)
    wq3 = w_uq.reshape(MLA_Q_LORA, MLA_HEADS, hd)
    wuq = pad_heads(wq3, hd).astype(BF16)
    q_scale = q_gain * hd ** -0.5
    wrot = wq3[:, :, MLA_NOPE:] * q_scale[MLA_NOPE:]
    wrot = jnp.concatenate([-wrot[:, :, MLA_ROPE // 2:], wrot[:, :, :MLA_ROPE // 2]], axis=-1)
    wur = jnp.pad(wrot, ((0, 0), (0, 0), (MLA_NOPE, LANES - hd))).reshape(MLA_Q_LORA, MLA_HEADS * LANES).astype(BF16)
    wkv = w_ukv.reshape(MLA_KV_LORA, MLA_HEADS, MLA_NOPE + MLA_V)
    wuk = pad_heads(wkv[:, :, :MLA_NOPE], MLA_NOPE).astype(BF16)
    wuv = wkv[:, :, MLA_NOPE:].reshape(MLA_KV_LORA, MLA_HEADS * MLA_V).astype(BF16)
    qg = jnp.pad(q_scale, (0, LANES - hd)).reshape(1, LANES)
    kg = jnp.pad(k_gain, (0, LANES - hd)).reshape(1, LANES)
    cos_t, sin_up, sin_dn = _rope_tables(s, MLA_ROPE, MLA_NOPE, LANES)
    gain_cos = cos_t * qg
    sin_abs = sin_dn - sin_up
    tiles = _tiles(t)
    tm = min(tiles["mla_proj"], s)
    nt = s // tm
    nq = MLA_HEADS * LANES
    nv = MLA_HEADS * MLA_V
    full = lambda shape: pl.BlockSpec(shape, lambda i: (0, 0))
    tab = pl.BlockSpec((tm, LANES), lambda i: (i % nt, 0))
    q, k, v = pl.pallas_call(
        _mla_proj_kernel,
        grid=(t // tm,),
        in_specs=[
            pl.BlockSpec((tm, d), lambda i: (i, 0)),
            full((1, d)), full((d, MLA_DOWN_PAD)), full((1, MLA_Q_LORA)), full((1, MLA_KV_LORA)),
            full((MLA_Q_LORA, nq)), full((MLA_Q_LORA, nq)), full((MLA_KV_LORA, nq)), full((MLA_KV_LORA, nv)),
            full((1, LANES)), tab, tab, tab, tab, tab,
        ],
        out_specs=[
            pl.BlockSpec((tm, nq), lambda i: (i, 0)),
            pl.BlockSpec((tm, nq), lambda i: (i, 0)),
            pl.BlockSpec((tm, nv), lambda i: (i, 0)),
        ],
        out_shape=[jax.ShapeDtypeStruct((t, nq), BF16), jax.ShapeDtypeStruct((t, nq), BF16),
                   jax.ShapeDtypeStruct((t, nv), BF16)],
        compiler_params=_params("parallel"),
        name="mla_proj",
    )(x.reshape(t, d), g.reshape(1, d), wd, cq_gain.reshape(1, -1), ckv_gain.reshape(1, -1), wuq, wur, wuk, wuv,
      kg, gain_cos, sin_abs, cos_t, sin_up, sin_dn)

    tq = min(tiles["mla_q"], s)
    n_q = s // tq
    pp = MLA_PAIRS_PER_STEP
    o = pl.pallas_call(
        _mla_attn_kernel,
        grid=(b, MLA_HEADS // (2 * pp), n_q + 1),
        in_specs=[
            pl.BlockSpec((1, tq, 2 * pp * LANES), lambda bi, p, i: (bi, jnp.minimum(i, n_q - 1), p)),
            pl.BlockSpec((1, s, 2 * pp * LANES), lambda bi, p, i: (bi, 0, p)),
            pl.BlockSpec((1, s, pp * LANES), lambda bi, p, i: (bi, 0, p)),
        ],
        out_specs=pl.BlockSpec((1, tq, pp * LANES), lambda bi, p, i: (bi, jnp.maximum(i - 1, 0), p)),
        out_shape=jax.ShapeDtypeStruct((b, s, nv), BF16),
        scratch_shapes=[pltpu.VMEM((2 * pp, s, tq), F32), pltpu.VMEM((2 * pp, 1, tq), F32)],
        compiler_params=_params("parallel", "parallel", "arbitrary"),
        name="mla_attention",
    )(q.reshape(b, s, nq), k.reshape(b, s, nq), v.reshape(b, s, nv))
    return _proj_res(o.reshape(t, nv), w_o, x.reshape(t, d)).reshape(b, s, d)


def _rwkv_proj_kernel(x_ref, xp_ref, xn_ref, g_ref, mu_ref, wr_ref, wk_ref, wv_ref, g1_ref, g2_ref,
                      w1_ref, w2_ref, w0_ref, a1_ref, a2_ref, a0_ref,
                      r_ref, k_ref, v_ref, gt_ref, lwf_ref, lwb_ref, af_ref, ab_ref, *, tiles_per_seq):
    i = pl.program_id(0)
    gain = g_ref[...]
    h = _rms(x_ref[...]) * gain
    tm = h.shape[0]
    first = (i % tiles_per_seq) == 0
    last = (i % tiles_per_seq) == tiles_per_seq - 1
    h_before = jnp.where(first, 0.0, _rms(xp_ref[7:8, :]) * gain)
    h_after = jnp.where(last, 0.0, _rms(xn_ref[0:1, :]) * gain)
    row = lax.broadcasted_iota(jnp.int32, h.shape, 0)
    h_prev = jnp.where(row == 0, h_before, pltpu.roll(h, 1, 0))
    h_next = jnp.where(row == tm - 1, h_after, pltpu.roll(h, tm - 1, 0))
    xx = 0.5 * (h_prev + h_next) - h
    mix = lambda n: (h + xx * mu_ref[n:n + 1, :]).astype(BF16)
    r_ref[...] = _dot(mix(0), wr_ref[...]).astype(r_ref.dtype)
    k_ref[...] = _dot(mix(2), wk_ref[...]).astype(k_ref.dtype)
    v_ref[...] = _dot(mix(3), wv_ref[...]).astype(v_ref.dtype)
    gt_ref[...] = _dot(jax.nn.sigmoid(_dot(mix(5), g1_ref[...])).astype(BF16), g2_ref[...]).astype(gt_ref.dtype)
    d = h.shape[1]
    wl = w0_ref[...] + _dot(jnp.tanh(_dot(mix(1), w1_ref[...])).astype(BF16), w2_ref[...])
    lw = -jnp.exp(-0.5) * jax.nn.sigmoid(wl)
    lwf_ref[...] = lw[:, :d]
    lwb_ref[...] = lw[:, d:]
    al = jax.nn.sigmoid(a0_ref[...] + _dot(_dot(mix(4), a1_ref[...]).astype(BF16), a2_ref[...]))
    af_ref[...] = al[:, :d].astype(af_ref.dtype)
    ab_ref[...] = al[:, d:].astype(ab_ref.dtype)


def _wkv_masks(c, rev):
    n = 2 * WKV_PAIRS * c
    r = lax.broadcasted_iota(jnp.int32, (n, WKV_PAIRS * c), 0)
    q = lax.broadcasted_iota(jnp.int32, (n, WKV_PAIRS * c), 1)
    i, j = r % c, q % c
    same_pair = ((r // c) % WKV_PAIRS) == (q // c)
    upto = same_pair & ((j >= i) if rev else (j <= i))
    before = same_pair & ((j > i) if rev else (j < i))
    half = r // (WKV_PAIRS * c)
    rn = lax.broadcasted_iota(jnp.int32, (n, n), 0)
    cn = lax.broadcasted_iota(jnp.int32, (n, n), 1)
    ns = WKV_PAIRS * LANES
    r2 = lax.broadcasted_iota(jnp.int32, (ns, ns), 0)
    c2 = lax.broadcasted_iota(jnp.int32, (ns, ns), 1)
    ri = lax.broadcasted_iota(jnp.int32, (c, c), 0)
    ci = lax.broadcasted_iota(jnp.int32, (c, c), 1)
    sizes = []
    s = 2
    while s < c:
        sizes.append(s)
        s *= 2
    return {
        "upto": [upto & (half == m) for m in range(2)],
        "before": [before & (half == m) for m in range(2)],
        "join": [((rn // (2 * s)) == (cn // (2 * s))) & ((rn // s) != (cn // s)) for s in sizes],
        "base": (rn // 2) == (cn // 2),
        "eye": jnp.where(rn == cn, 1.0, 0.0),
        "diag": r2 == c2,
        "same_head": (r2 // HEAD_DIM) == (c2 // HEAD_DIM),
        "cum": jnp.where((ci >= ri) if rev else (ci <= ri), 1.0, 0.0),
    }


def _cat_l(parts):
    return jnp.concatenate(parts, axis=1)


def _cat_r(parts):
    return jnp.concatenate(parts, axis=0)


def _pair(t, p):
    return t[:, p * LANES:(p + 1) * LANES]


def _stack_pairs(t, reps):
    return _cat_r([_pair(t, p) for _ in range(reps) for p in range(WKV_PAIRS)])


def _wkv_prepare(r, lw, k, v, asig, k_k, k_a, rev, masks):
    c = r.shape[0]
    m0p = _low_half((c, LANES))
    pairs = range(WKV_PAIRS)
    pair, cat_l, cat_r = _pair, _cat_l, _cat_r

    kk = k * k_k
    kk2 = kk * kk
    kk = kk / jnp.maximum(jnp.sqrt(cat_l([_pair_sum(pair(kk2, p), m0p) for p in pairs])), 1e-12)
    bvec = kk * asig
    kd = k * (1.0 + (asig - 1.0) * k_a)

    cl = _mm_exact_lhs(masks["cum"], lw)
    tot = cl[0:1, :] if rev else cl[c - 1:c, :]
    g_inv = jnp.exp(-cl)
    g_end = jnp.exp(tot - cl)
    at = -kk * jnp.exp(cl - lw)
    rt = r * jnp.exp(cl)
    bt = bvec * g_inv
    kt = kd * g_inv
    b_end = bvec * g_end
    k_end = kd * g_end

    lane = lax.broadcasted_iota(jnp.int32, r.shape, 1)
    low = lane % LANES < HEAD_DIM
    lhs = cat_r([jnp.where(low, at, 0.0), jnp.where(low, 0.0, at), jnp.where(low, rt, 0.0), jnp.where(low, 0.0, rt)])
    rhs = cat_r([jnp.where(lane // LANES == p, t, 0.0) for t in (bt, kt) for p in pairs])
    sc = _mm(lhs, rhs, NT, WKV_SCORE_PASSES)
    pc = WKV_PAIRS * c

    def block_diag(lhs_kind, rhs_kind, key):
        s = sc[lhs_kind * 2 * c:(lhs_kind + 1) * 2 * c, rhs_kind * pc:(rhs_kind + 1) * pc]
        st = cat_r([s[m * c:(m + 1) * c] for m in range(2) for _ in pairs])
        return cat_l([jnp.where(masks[key][m], st, 0.0) for m in range(2)])

    return {"a_ab": block_diag(0, 0, "before"), "a_rb": block_diag(1, 0, "upto"),
            "a_ak": block_diag(0, 1, "before"), "a_rk": block_diag(1, 1, "upto"),
            "at": at, "rt": rt, "v": v, "b_end": b_end, "k_end": k_end, "decay": jnp.exp(tot)}


def _wkv_chunks(units):
    pres = [u[0] for u in units]
    states = [u[1] for u in units]
    masks = [u[2] for u in units]
    c = pres[0]["at"].shape[0]
    m0p = _low_half((c, LANES))
    inv = [mk["eye"] + jnp.where(mk["base"], p["a_ab"], 0.0) for p, mk in zip(pres, masks)]
    for lvl in range(len(masks[0]["join"])):
        ex = [_mm(jnp.where(mk["join"][lvl], p["a_ab"], 0.0), x, NN, WKV_INV_PASSES)
              for p, mk, x in zip(pres, masks, inv)]
        inv = [x + _mm(x, t, NN, WKV_INV_PASSES) for x, t in zip(inv, ex)]
    v_st = [_stack_pairs(p["v"], 2) for p in pres]
    x_ak = [_mm(p["a_ak"], vs) for p, vs in zip(pres, v_st)]
    yk = [_mm(p["a_rk"], vs) for p, vs in zip(pres, v_st)]
    zz = [_mm(x, _cat_l([_stack_pairs(p["at"], 2), xa]), NN, WKV_INV_PASSES)
          for x, p, xa in zip(inv, pres, x_ak)]

    def unstack(t):
        return _cat_l([jnp.where(m0p, t[p * c:(p + 1) * c], t[(WKV_PAIRS + p) * c:(WKV_PAIRS + p + 1) * c])
                       for p in range(WKV_PAIRS)])

    w_a = [unstack(z[:, :LANES]) for z in zz]
    u_0 = [unstack(z[:, LANES:]) for z in zz]
    hw = [_mm(_cat_r([w, p["rt"]]), st, NN, WKV_STATE_PASSES) for w, p, st in zip(w_a, pres, states)]
    u = [h[:c] + u0 for h, u0 in zip(hw, u_0)]
    yu = [_mm(p["a_rb"], _stack_pairs(uu, 2)) for p, uu in zip(pres, u)]
    grow = [_mm(_cat_r([p["b_end"], p["k_end"]]), _cat_r([uu, p["v"]]), TN) for p, uu in zip(pres, u)]
    out = []
    for p, mk, st, h, yuu, ykk, gr in zip(pres, masks, states, hw, yu, yk, grow):
        y = h[c:] + unstack(yuu + ykk)
        decay_col = jnp.sum(jnp.where(mk["diag"], p["decay"], 0.0), axis=-1, keepdims=True)
        out.append((y, st * decay_col + jnp.where(mk["same_head"], gr, 0.0)))
    return out


def _rwkv_scan_kernel(rf_ref, kf_ref, vf_ref, rb_ref, kb_ref, vb_ref, lwf_ref, af_ref, lwb_ref, ab_ref,
                      kk_ref, ka_ref, yf_ref, yb_ref, st_ref, *, groups):
    @pl.when(pl.program_id(2) == 0)
    def _():
        st_ref[...] = jnp.zeros_like(st_ref)

    c = rf_ref.shape[1]
    width = WKV_PAIRS * LANES
    dirs = ((rf_ref, kf_ref, vf_ref, lwf_ref, af_ref, False), (rb_ref, kb_ref, vb_ref, lwb_ref, ab_ref, True))
    units = []
    for di, (r_ref, k_ref, v_ref, lw_ref, a_ref, rev) in enumerate(dirs):
        masks = _wkv_masks(c, rev)
        for gi in range(groups):
            sl = slice(gi * width, (gi + 1) * width)
            f32 = lambda ref: ref[0, :, sl].astype(F32)
            pre = _wkv_prepare(f32(r_ref), lw_ref[0, :, sl], f32(k_ref), f32(v_ref), f32(a_ref),
                               kk_ref[:, sl], ka_ref[:, sl], rev, masks)
            units.append((pre, st_ref[di, gi], masks))
    results = _wkv_chunks(units)
    for ui, (y, st) in enumerate(results):
        di, gi = divmod(ui, groups)
        (yf_ref, yb_ref)[di][0, :, gi * width:(gi + 1) * width] = y.astype(yf_ref.dtype)
        st_ref[di, gi] = st


def _rwkv_out_kernel(x_ref, r_ref, k_ref, v_ref, af_ref, ab_ref, gt_ref, yf_ref, yb_ref,
                     ka_ref, rk_ref, lw_ref, lb_ref, wo_ref, o_ref, act_ref):
    inv_n = 1.0 / HEAD_DIM
    r2 = lax.broadcasted_iota(jnp.int32, (LANES, LANES), 0) // HEAD_DIM
    c2 = lax.broadcasted_iota(jnp.int32, (LANES, LANES), 1) // HEAD_DIM
    head_ones = jnp.where(r2 == c2, 1.0, 0.0).astype(BF16)

    def head_sum(t, passes):
        hi = t.astype(BF16)
        out = _dot(hi, head_ones)
        if passes == 2:
            out = out + _dot((t - hi.astype(F32)).astype(BF16), head_ones)
        return out

    for c in range(x_ref.shape[1] // LANES):
        sl = slice(c * LANES, (c + 1) * LANES)
        f32 = lambda ref: ref[:, sl].astype(F32)
        r, k, v = f32(r_ref), f32(k_ref), f32(v_ref)
        k_a = ka_ref[:, sl]
        kd_both = k * (2.0 + (f32(af_ref) + f32(ab_ref) - 2.0) * k_a)
        bonus = head_sum(r * rk_ref[:, sl] * kd_both, 2) * v
        y = f32(yf_ref) + f32(yb_ref)
        dev = y - head_sum(y, 2) * inv_n
        var = head_sum(dev * dev, 1) * inv_n
        yn = dev * lax.rsqrt(var + RWKV_GN_EPS) * lw_ref[:, sl] + lb_ref[:, sl]
        act_ref[:, sl] = ((yn + bonus) * gt_ref[:, sl]).astype(BF16)
    o_ref[...] = x_ref[...] + _dot(act_ref[...], wo_ref[...])


def _rwkv_layer(x, g, mu, w_r, w_k, w_v, w0, w1, w2, a0, a1, a2, g1, g2, k_k, k_a, r_k, lnx_w, lnx_b, w_o):
    b, s, d = x.shape
    t = b * s
    tiles = _tiles(t)
    tm = min(tiles["rwkv_proj"], s)
    xf = x.reshape(t, d)
    lora_g = g1.shape[1]
    gpad = -lora_g % LANES
    g1p = jnp.pad(g1, ((0, 0), (0, gpad))).astype(BF16)
    g2p = jnp.pad(g2, ((0, gpad), (0, 0))).astype(BF16)

    def both_dirs(m1, m2):
        l = m1.shape[2]
        z = jnp.zeros((l, d), m2.dtype)
        cat = jnp.concatenate([m1[0], m1[1]], axis=1)
        bd = jnp.concatenate([jnp.concatenate([m2[0], z], axis=1), jnp.concatenate([z, m2[1]], axis=1)], axis=0)
        return cat.astype(BF16), bd.astype(BF16)

    w1c, w2c = both_dirs(w1, w2)
    a1c, a2c = both_dirs(a1, a2)
    full = lambda arr: pl.BlockSpec(arr.shape, lambda i: (0,) * arr.ndim)
    halo = tm // 8
    n8 = t // 8
    tok = pl.BlockSpec((tm, d), lambda i: (i, 0))
    weights = [g.reshape(1, d), mu, w_r.astype(BF16), w_k.astype(BF16), w_v.astype(BF16), g1p, g2p,
               w1c, w2c, w0.reshape(1, 2 * d), a1c, a2c, a0.reshape(1, 2 * d)]
    outs = pl.pallas_call(
        functools.partial(_rwkv_proj_kernel, tiles_per_seq=s // tm),
        grid=(t // tm,),
        in_specs=[tok,
                  pl.BlockSpec((8, d), lambda i: (jnp.maximum(i * halo - 1, 0), 0)),
                  pl.BlockSpec((8, d), lambda i: (jnp.minimum((i + 1) * halo, n8 - 1), 0)),
                  ] + [full(w) for w in weights],
        out_specs=[tok] * 8,
        out_shape=[jax.ShapeDtypeStruct((t, d), dt) for dt in (BF16, BF16, BF16, BF16, F32, F32, BF16, BF16)],
        compiler_params=_params("parallel"),
        name="rwkv_proj",
    )(xf, xf, xf, *weights)
    r, k, v, gate, lw_f, lw_b, a_f, a_b = [o.reshape(b, s, d) for o in outs]

    c = RWKV_CHUNK
    nc = s // c
    groups = WKV_GROUPS_PER_STEP
    wpair = groups * WKV_PAIRS * LANES
    n_state = 2 * WKV_PAIRS * HEAD_DIM
    fwd = pl.BlockSpec((1, c, wpair), lambda bi, p, ci: (bi, ci, p))
    bwd = pl.BlockSpec((1, c, wpair), lambda bi, p, ci: (bi, nc - 1 - ci, p))
    vec = pl.BlockSpec((1, wpair), lambda bi, p, ci: (0, p))
    y_f, y_b = pl.pallas_call(
        functools.partial(_rwkv_scan_kernel, groups=groups),
        grid=(b, d // wpair, nc),
        in_specs=[fwd, fwd, fwd, bwd, bwd, bwd, fwd, fwd, bwd, bwd, vec, vec],
        out_specs=[fwd, bwd],
        out_shape=[jax.ShapeDtypeStruct((b, s, d), BF16)] * 2,
        scratch_shapes=[pltpu.VMEM((2, groups, n_state, n_state), F32)],
        compiler_params=_params("parallel", "parallel", "arbitrary"),
        name="rwkv_scan",
    )(r, k, v, r, k, v, lw_f, a_f, lw_b, a_b, k_k.reshape(1, d), k_a.reshape(1, d))

    flat = lambda z: z.reshape(t, d)
    rowv = lambda z: z.reshape(1, d)
    tmo = tiles["rwkv_proj"]
    tok_o = pl.BlockSpec((tmo, d), lambda i: (i, 0))
    vec_o = pl.BlockSpec((1, d), lambda i: (0, 0))
    out = pl.pallas_call(
        _rwkv_out_kernel,
        grid=(t // tmo,),
        in_specs=[tok_o] * 9 + [vec_o] * 4 + [pl.BlockSpec((d, d), lambda i: (0, 0))],
        out_specs=tok_o,
        out_shape=jax.ShapeDtypeStruct((t, d), F32),
        scratch_shapes=[pltpu.VMEM((tmo, d), BF16)],
        compiler_params=_params("parallel"),
        name="rwkv_out",
    )(xf, flat(r), flat(k), flat(v), flat(a_f), flat(a_b), flat(gate), flat(y_f), flat(y_b),
      rowv(k_a), rowv(r_k), rowv(lnx_w), rowv(lnx_b), w_o.astype(BF16))
    return out.reshape(b, s, d)


def kernel(x, norm_tok, norm_ch, ffn_w_up, ffn_conv_w, ffn_conv_b, ffn_w_down, swa_w_qkv, swa_q_gain, swa_k_gain, swa_sinks, swa_w_o, rwkv_mu, rwkv_w_r, rwkv_w_k, rwkv_w_v, rwkv_w0, rwkv_w1, rwkv_w2, rwkv_a0, rwkv_a1, rwkv_a2, rwkv_g1, rwkv_g2, rwkv_k_k, rwkv_k_a, rwkv_r_k, rwkv_lnx_w, rwkv_lnx_b, rwkv_w_o, mla_w_down, mla_cq_gain, mla_ckv_gain, mla_w_uq, mla_w_ukv, mla_q_gain, mla_k_gain, mla_w_o):
    depth = norm_tok.shape[0]
    for i in range(depth):
        kind = i % N_MIXERS
        j = i // N_MIXERS
        if kind == 0:
            x = _swa_layer(x, norm_tok[i], swa_w_qkv[j], swa_q_gain[j], swa_k_gain[j], swa_sinks[j], swa_w_o[j])
        elif kind == 1:
            x = _rwkv_layer(x, norm_tok[i], rwkv_mu[j], rwkv_w_r[j], rwkv_w_k[j], rwkv_w_v[j], rwkv_w0[j],
                            rwkv_w1[j], rwkv_w2[j], rwkv_a0[j], rwkv_a1[j], rwkv_a2[j], rwkv_g1[j], rwkv_g2[j],
                            rwkv_k_k[j], rwkv_k_a[j], rwkv_r_k[j], rwkv_lnx_w[j], rwkv_lnx_b[j], rwkv_w_o[j])
        else:
            x = _mla_layer(x, norm_tok[i], mla_w_down[j], mla_cq_gain[j], mla_ckv_gain[j], mla_w_uq[j],
                           mla_w_ukv[j], mla_q_gain[j], mla_k_gain[j], mla_w_o[j])
        x = _ffn(x, norm_ch[i], ffn_w_up[i], ffn_conv_w[i], ffn_conv_b[i], ffn_w_down[i])
    return x
```
